```python
import math
import jax, jax.numpy as jnp
from jax import lax
import numpy as np

D_MODEL = 1024
BATCH = 8
SEQ = 2048
DEPTH = 2
DEC_BATCH = 128
DEC_SEQ = 1
PAST_LEN = 16384
PAGE_SIZE = 128

S5_WIDTH = D_MODEL // 2
S5_GROUP = 16
S5_GROUPS = S5_WIDTH // S5_GROUP
S5_STATE = 64
DT_MIN = 0.001
DT_MAX = 0.1
HG_WIDTH = D_MODEL // 2
HG_HEAD_DIM = 128
HG_HEADS = HG_WIDTH // HG_HEAD_DIM
HG_CHUNK = 16
RMS_EPS = 1e-6
N_IN = S5_WIDTH + 4 * HG_WIDTH + 2 * D_MODEL
IN_SPLITS = (S5_WIDTH, S5_WIDTH + HG_WIDTH, S5_WIDTH + 2 * HG_WIDTH,
             S5_WIDTH + 3 * HG_WIDTH, S5_WIDTH + 4 * HG_WIDTH)
N_EXPERTS = 32
TOP_K = 4
D_EXPERT = D_MODEL
SWIGLU_LIMIT = 7.0
SWIGLU_ALPHA = 1.702
MOE_BLOCK = 128
DEEPNORM_ALPHA = (2 * DEPTH) ** 0.25
DEEPNORM_BETA = (8 * DEPTH) ** -0.25
LN_EPS = 1e-5

kernel_name = 'hybrid_s5_hgrn2_moe_decoder_step'


def layer_norm(x, g, b):
    xf = x.astype(jnp.float32)
    mu = jnp.mean(xf, -1, keepdims=True)
    var = jnp.mean(jnp.square(xf - mu), -1, keepdims=True)
    return ((xf - mu) * lax.rsqrt(var + LN_EPS) * g.astype(jnp.float32)
            + b.astype(jnp.float32)).astype(x.dtype)


def s5_branch(u, h0_re, h0_im, lam_re, lam_im, log_dt, b_re, b_im, c_re, c_im, d_skip, w_glu, b_glu):
    bsz, length, _ = u.shape
    f32 = jnp.float32
    uf = u.astype(f32).reshape(bsz, length, S5_GROUPS, S5_GROUP)
    lam_re, lam_im = lam_re.astype(f32), lam_im.astype(f32)
    dt = jnp.exp(log_dt.astype(f32))[:, None]
    mag = jnp.exp(lam_re * dt)
    ang = lam_im * dt
    ab_re, ab_im = mag * jnp.cos(ang), mag * jnp.sin(ang)
    den = jnp.square(lam_re) + jnp.square(lam_im)
    nr, ni = ab_re - 1.0, ab_im
    zf_re = (nr * lam_re + ni * lam_im) / den
    zf_im = (ni * lam_re - nr * lam_im) / den
    b_re, b_im = b_re.astype(f32), b_im.astype(f32)
    bb_re = zf_re[..., None] * b_re - zf_im[..., None] * b_im
    bb_im = zf_re[..., None] * b_im + zf_im[..., None] * b_re
    bu_re = jnp.einsum('blgh,gph->blgp', uf, bb_re)
    bu_im = jnp.einsum('blgh,gph->blgp', uf, bb_im)
    h0_re, h0_im = h0_re.astype(f32), h0_im.astype(f32)
    bu_re = bu_re.at[:, 0].add(ab_re * h0_re - ab_im * h0_im)
    bu_im = bu_im.at[:, 0].add(ab_re * h0_im + ab_im * h0_re)
    a_re = jnp.broadcast_to(ab_re, bu_re.shape)
    a_im = jnp.broadcast_to(ab_im, bu_im.shape)

    def combine(e1, e2):
        a1r, a1i, b1r, b1i = e1
        a2r, a2i, b2r, b2i = e2
        return (a2r * a1r - a2i * a1i, a2r * a1i + a2i * a1r,
                a2r * b1r - a2i * b1i + b2r, a2r * b1i + a2i * b1r + b2i)

    _, _, h_re, h_im = lax.associative_scan(combine, (a_re, a_im, bu_re, bu_im), axis=1)
    y = (jnp.einsum('blgp,ghp->blgh', h_re, c_re.astype(f32))
         - jnp.einsum('blgp,ghp->blgh', h_im, c_im.astype(f32))
         + d_skip.astype(f32) * uf)
    y = jax.nn.gelu(y.reshape(bsz, length, S5_WIDTH))
    y = y * jax.nn.sigmoid(y @ w_glu.astype(f32) + b_glu.astype(f32))
    return y, h_re[:, -1], h_im[:, -1]


def hgrn2_recurrence(q, k, v, log_f, s0):
    bsz, length = q.shape[0], q.shape[1]
    n_chunks = -(-length // HG_CHUNK)
    pad = n_chunks * HG_CHUNK - length

    def to_chunks(t):
        t = jnp.pad(t, ((0, 0), (0, pad), (0, 0), (0, 0)))
        t = t.reshape(bsz, n_chunks, HG_CHUNK, HG_HEADS, t.shape[-1])
        return jnp.moveaxis(t, 1, 0)

    causal = jnp.tril(jnp.ones((HG_CHUNK, HG_CHUNK), bool))[None, :, :, None, None]

    def step(s, chunk):
        qc, kc, vc, gc = chunk
        b = jnp.cumsum(gc, axis=1)
        o_inter = jnp.einsum('bthk,bhkv->bthv', qc * jnp.exp(b), s)
        diff = b[:, :, None] - b[:, None, :]
        decay = jnp.exp(jnp.where(causal, diff, -jnp.inf))
        att = jnp.einsum('bthk,bshk,btshk->bths', qc, kc, decay)
        o_intra = jnp.einsum('bths,bshv->bthv', att, vc)
        b_last = b[:, -1]
        k_dec = kc * jnp.exp(b_last[:, None] - b)
        s_new = jnp.exp(b_last)[..., None] * s + jnp.einsum('bshk,bshv->bhkv', k_dec, vc)
        return s_new, o_inter + o_intra

    xs = (to_chunks(q), to_chunks(k), to_chunks(v), to_chunks(log_f))
    s_final, o = lax.scan(step, s0, xs)
    o = jnp.moveaxis(o, 0, 1).reshape(bsz, n_chunks * HG_CHUNK, HG_HEADS, HG_HEAD_DIM)[:, :length]
    return o, s_final


def hgrn2_branch(q, f_pre, v, g_out, s0, lb, gnorm_w, first_layer):
    bsz, length, _ = q.shape
    f32 = jnp.float32

    def heads(t):
        return t.astype(f32).reshape(bsz, length, HG_HEADS, HG_HEAD_DIM)

    q, f_pre, v = heads(q), heads(f_pre), heads(v)
    if first_layer:
        log_f = jax.nn.log_sigmoid(f_pre)
        k = jax.nn.sigmoid(-f_pre)
    else:
        lbh = lb.astype(f32).reshape(HG_HEADS, HG_HEAD_DIM)
        log_f = jnp.log(lbh + (1.0 - lbh) * jax.nn.sigmoid(f_pre))
        k = (1.0 - lbh) * jax.nn.sigmoid(-f_pre)
    o, s_final = hgrn2_recurrence(q, k, v, log_f, s0.astype(f32))
    o = o * lax.rsqrt(jnp.mean(jnp.square(o), -1, keepdims=True) + RMS_EPS) * gnorm_w.astype(f32)
    o = o.reshape(bsz, length, HG_WIDTH) * jax.nn.silu(g_out.astype(f32))
    return o, s_final


def moe_ffn(u, router_w, router_b, w_gate_up, b_gate_up, w_down, b_down):
    n_tok, d = u.shape
    logits = u.astype(jnp.float32) @ router_w.astype(jnp.float32) + router_b.astype(jnp.float32)
    top_val, top_idx = lax.top_k(logits, TOP_K)
    top_w = jax.nn.softmax(top_val, axis=-1)
    n_assign = n_tok * TOP_K
    flat_e = top_idx.reshape(-1)
    flat_t = jnp.repeat(jnp.arange(n_tok, dtype=jnp.int32), TOP_K)
    order = jnp.argsort(flat_e)
    se, st, sw = flat_e[order], flat_t[order], top_w.reshape(-1)[order]
    counts = jnp.bincount(flat_e, length=N_EXPERTS)
    padded = (counts + MOE_BLOCK - 1) // MOE_BLOCK * MOE_BLOCK
    pad_end = jnp.cumsum(padded)
    pad_start = pad_end - padded
    start = jnp.cumsum(counts) - counts
    dest = pad_start[se] + jnp.arange(n_assign, dtype=jnp.int32) - start[se]
    n_blocks = -(-n_assign // MOE_BLOCK) + N_EXPERTS
    slot_tok = jnp.full((n_blocks * MOE_BLOCK,), n_tok, jnp.int32).at[dest].set(st)
    block_e = jnp.minimum(jnp.searchsorted(pad_end, jnp.arange(n_blocks) * MOE_BLOCK, side='right'),
                          N_EXPERTS - 1)
    u_pad = jnp.concatenate([u, jnp.zeros((1, d), u.dtype)], axis=0)
    xb = u_pad[slot_tok].reshape(n_blocks, MOE_BLOCK, d)

    def expert_block(args):
        xblk, e = args
        h = xblk @ w_gate_up[e] + b_gate_up[e]
        gate, up = h[:, :D_EXPERT], h[:, D_EXPERT:]
        gate = jnp.minimum(gate, SWIGLU_LIMIT)
        up = jnp.clip(up, -SWIGLU_LIMIT, SWIGLU_LIMIT)
        act = gate * jax.nn.sigmoid(SWIGLU_ALPHA * gate) * (up + 1.0)
        return act @ w_down[e] + b_down[e]

    yb = lax.map(expert_block, (xb, block_e)).reshape(n_blocks * MOE_BLOCK, d)
    y_assign = yb[dest] * sw[:, None].astype(yb.dtype)
    return jnp.zeros((n_tok, d), yb.dtype).at[st].add(y_assign).astype(u.dtype)


def trunk_layer(x, c, s5_re0, s5_im0, hg0, first_layer, lb, p):
    dt = x.dtype
    mod = jax.nn.silu(c) @ p['w_ada'] + p['b_ada']
    shift1, scale1, gate1, shift2, scale2, gate2 = jnp.split(mod[:, None, :].astype(dt), 6, axis=-1)
    u = x * (1 + scale1) + shift1
    proj = u @ p['w_in']
    u_s5, q, f_pre, i_in, g_out, gates = jnp.split(proj, IN_SPLITS, axis=-1)
    y_s5, s5_re, s5_im = s5_branch(u_s5, s5_re0, s5_im0, p['s5_lambda_re'], p['s5_lambda_im'],
                                   p['s5_log_dt'], p['s5_b_re'], p['s5_b_im'], p['s5_c_re'],
                                   p['s5_c_im'], p['s5_d'], p['w_glu'], p['b_glu'])
    y_hg, s_hg = hgrn2_branch(q, f_pre, i_in, g_out, hg0, lb, p['hgrn_gnorm_w'], first_layer)
    gate_s5, gate_hg = jnp.split(jax.nn.sigmoid(gates), 2, axis=-1)
    merged = (gate_s5 * (y_s5.astype(dt) @ p['w_up_s5'])
              + gate_hg * (y_hg.astype(dt) @ p['w_up_hg']))
    mix = merged @ p['w_out']
    x = layer_norm(DEEPNORM_ALPHA * x + gate1 * mix, p['ln1_g'], p['ln1_b'])
    u2 = x * (1 + scale2) + shift2
    ffn = moe_ffn(u2.reshape(-1, D_MODEL), p['router_w'], p['router_b'], p['w_gate_up'],
                  p['b_gate_up'], p['w_down'], p['b_down']).reshape(x.shape)
    x = layer_norm(DEEPNORM_ALPHA * x + gate2 * ffn, p['ln2_g'], p['ln2_b'])
    return x, s5_re, s5_im, s_hg


def setup_inputs(seed: int = 0) -> dict:
    key = jax.random.key(seed)
    ks = iter(jax.random.split(key, 64))

    def nrm(shape, scale):
        return scale * jax.random.normal(next(ks), shape, jnp.float32)

    d = D_MODEL
    n_idx = jnp.arange(S5_STATE, dtype=jnp.float32)
    return {
        'x_prompt': nrm((BATCH, SEQ, d), 1.0),
        'x_sample': nrm((DEC_BATCH, DEC_SEQ, d), 1.0),
        'state_s5_re': nrm((DEPTH, DEC_BATCH, S5_GROUPS, S5_STATE), 0.1),
        'state_s5_im': nrm((DEPTH, DEC_BATCH, S5_GROUPS, S5_STATE), 0.1),
        'state_hgrn': nrm((DEPTH, DEC_BATCH, HG_HEADS, HG_HEAD_DIM, HG_HEAD_DIM), 0.5),
        'c_prompt': nrm((BATCH, d), 1.0),
        'c_sample': nrm((DEC_BATCH, d), 1.0),
        'w_ada': nrm((DEPTH, d, 6 * d), d ** -0.5),
        'b_ada': nrm((DEPTH, 6 * d), 0.01),
        'w_in': nrm((DEPTH, d, N_IN), d ** -0.5),
        's5_lambda_re': -0.5 + nrm((DEPTH, S5_GROUPS, S5_STATE), 0.01),
        's5_lambda_im': math.pi * n_idx + nrm((DEPTH, S5_GROUPS, S5_STATE), 0.01),
        's5_log_dt': jax.random.uniform(next(ks), (DEPTH, S5_GROUPS), jnp.float32,
                                        math.log(DT_MIN), math.log(DT_MAX)),
        's5_b_re': nrm((DEPTH, S5_GROUPS, S5_STATE, S5_GROUP), (2 * S5_GROUP) ** -0.5),
        's5_b_im': nrm((DEPTH, S5_GROUPS, S5_STATE, S5_GROUP), (2 * S5_GROUP) ** -0.5),
        's5_c_re': nrm((DEPTH, S5_GROUPS, S5_GROUP, S5_STATE), 0.5),
        's5_c_im': nrm((DEPTH, S5_GROUPS, S5_GROUP, S5_STATE), 0.5),
        's5_d': nrm((DEPTH, S5_GROUPS, S5_GROUP), 0.5),
        'w_glu': nrm((DEPTH, S5_WIDTH, S5_WIDTH), S5_WIDTH ** -0.5),
        'b_glu': nrm((DEPTH, S5_WIDTH), 0.01),
        'w_up_s5': nrm((DEPTH, S5_WIDTH, d), S5_WIDTH ** -0.5),
        'hgrn_lb_logits': nrm((DEPTH, HG_WIDTH), 0.5),
        'hgrn_gnorm_w': 1.0 + nrm((DEPTH, HG_HEAD_DIM), 0.01),
        'w_up_hg': nrm((DEPTH, HG_WIDTH, d), HG_WIDTH ** -0.5),
        'w_out': nrm((DEPTH, d, d), d ** -0.5 * DEEPNORM_BETA),
        'ln1_g': 1.0 + nrm((DEPTH, d), 0.01),
        'ln1_b': nrm((DEPTH, d), 0.01),
        'router_w': nrm((DEPTH, d, N_EXPERTS), d ** -0.5),
        'router_b': nrm((DEPTH, N_EXPERTS), 0.01),
        'w_gate_up': nrm((DEPTH, N_EXPERTS, d, 2 * D_EXPERT), d ** -0.5),
        'b_gate_up': nrm((DEPTH, N_EXPERTS, 2 * D_EXPERT), 0.01),
        'w_down': nrm((DEPTH, N_EXPERTS, D_EXPERT, d), D_EXPERT ** -0.5 * DEEPNORM_BETA),
        'b_down': nrm((DEPTH, N_EXPERTS, d), 0.01),
        'ln2_g': 1.0 + nrm((DEPTH, d), 0.01),
        'ln2_b': nrm((DEPTH, d), 0.01),
    }


def reference(x_prompt, x_sample, state_s5_re, state_s5_im, state_hgrn, c_prompt, c_sample,
              w_ada, b_ada, w_in, s5_lambda_re, s5_lambda_im, s5_log_dt, s5_b_re, s5_b_im,
              s5_c_re, s5_c_im, s5_d, w_glu, b_glu, w_up_s5, hgrn_lb_logits, hgrn_gnorm_w,
              w_up_hg, w_out, ln1_g, ln1_b, router_w, router_b, w_gate_up, b_gate_up,
              w_down, b_down, ln2_g, ln2_b):
    lb_p = jax.nn.softmax(hgrn_lb_logits.astype(jnp.float32), axis=0)
    lb_all = jnp.cumsum(lb_p, axis=0) - lb_p[0]
    n_prompt = x_prompt.shape[0]
    zero_s5 = jnp.zeros((n_prompt, S5_GROUPS, S5_STATE), jnp.float32)
    zero_hg = jnp.zeros((n_prompt, HG_HEADS, HG_HEAD_DIM, HG_HEAD_DIM), jnp.float32)
    yp, ys = x_prompt, x_sample
    p_re, p_im, p_hg, s_re, s_im, s_hg = [], [], [], [], [], []
    for l in range(DEPTH):
        p = {'w_ada': w_ada[l], 'b_ada': b_ada[l], 'w_in': w_in[l],
             's5_lambda_re': s5_lambda_re[l], 's5_lambda_im': s5_lambda_im[l],
             's5_log_dt': s5_log_dt[l], 's5_b_re': s5_b_re[l], 's5_b_im': s5_b_im[l],
             's5_c_re': s5_c_re[l], 's5_c_im': s5_c_im[l], 's5_d': s5_d[l],
             'w_glu': w_glu[l], 'b_glu': b_glu[l], 'w_up_s5': w_up_s5[l],
             'hgrn_gnorm_w': hgrn_gnorm_w[l], 'w_up_hg': w_up_hg[l], 'w_out': w_out[l],
             'ln1_g': ln1_g[l], 'ln1_b': ln1_b[l], 'router_w': router_w[l],
             'router_b': router_b[l], 'w_gate_up': w_gate_up[l], 'b_gate_up': b_gate_up[l],
             'w_down': w_down[l], 'b_down': b_down[l], 'ln2_g': ln2_g[l], 'ln2_b': ln2_b[l]}
        first = l == 0
        yp, a, b, c = trunk_layer(yp, c_prompt, zero_s5, zero_s5, zero_hg, first, lb_all[l], p)
        p_re.append(a)
        p_im.append(b)
        p_hg.append(c)
        ys, a, b, c = trunk_layer(ys, c_sample, state_s5_re[l], state_s5_im[l], state_hgrn[l],
                                  first, lb_all[l], p)
        s_re.append(a)
        s_im.append(b)
        s_hg.append(c)
    return (yp, ys, jnp.stack(p_re), jnp.stack(p_im), jnp.stack(p_hg),
            jnp.stack(s_re), jnp.stack(s_im), jnp.stack(s_hg))
```

```python
import functools

import jax
import jax.numpy as jnp
from jax import lax
from jax.experimental import pallas as pl
from jax.experimental.pallas import tpu as pltpu

F32 = jnp.float32
BF16 = jnp.bfloat16

D_MODEL = 1024
DEPTH = 2
S5_WIDTH = 512
S5_GROUP = 16
S5_GROUPS = 32
S5_STATE = 64
S5_FLAT = S5_GROUPS * S5_STATE
HG_WIDTH = 512
HG_HEAD_DIM = 128
HG_HEADS = 4
HG_CHUNK = 16
RMS_EPS = 1e-6
N_IN = S5_WIDTH + 4 * HG_WIDTH + 2 * D_MODEL
N_EXPERTS = 32
TOP_K = 4
D_EXPERT = D_MODEL
SWIGLU_LIMIT = 7.0
SWIGLU_ALPHA = 1.702
MOE_BLOCK = 128
DEEPNORM_ALPHA = (2 * DEPTH) ** 0.25
LN_EPS = 1e-5

LANES = 128
SUBLANES = 8
VMEM_LIMIT = 56 * 1024 * 1024
NEG_BIG = -1e30


def _cparams(*sem):
    return pltpu.CompilerParams(dimension_semantics=sem, vmem_limit_bytes=VMEM_LIMIT)


def _layer_norm(x, g, b):
    mu = jnp.mean(x, axis=-1, keepdims=True)
    xc = x - mu
    var = jnp.mean(xc * xc, axis=-1, keepdims=True)
    return xc * lax.rsqrt(var + LN_EPS) * g + b


def _mod_kernel(c_ref, w_ref, b_ref, o_ref):
    c = c_ref[...]
    s = c * jax.nn.sigmoid(c)
    o_ref[0] = jnp.dot(s.astype(BF16), w_ref[0].astype(BF16),
                       preferred_element_type=F32) + b_ref[0]


def _modulation(c_all, w_ada, b_ada):
    n_rows = c_all.shape[0]
    tn = 1536
    return pl.pallas_call(
        _mod_kernel,
        grid=(DEPTH, 6 * D_MODEL // tn),
        in_specs=[
            pl.BlockSpec((n_rows, D_MODEL), lambda l, j: (0, 0)),
            pl.BlockSpec((1, D_MODEL, tn), lambda l, j: (l, 0, j)),
            pl.BlockSpec((1, 1, tn), lambda l, j: (l, 0, j)),
        ],
        out_specs=pl.BlockSpec((1, n_rows, tn), lambda l, j: (l, 0, j)),
        out_shape=jax.ShapeDtypeStruct((DEPTH, n_rows, 6 * D_MODEL), F32),
        compiler_params=_cparams("parallel", "parallel"),
        name="adaln_mod",
    )(c_all, w_ada, b_ada.reshape(DEPTH, 1, 6 * D_MODEL))


def _proj_kernel(x_ref, shift_ref, scale_ref, w_ref, us5_ref, hg_ref, gates_ref):
    u = x_ref[0] * (1.0 + scale_ref[0]) + shift_ref[0]
    ub = u.astype(BF16)
    c0, c1 = S5_WIDTH, S5_WIDTH + 4 * HG_WIDTH
    us5_ref[0] = jnp.dot(ub, w_ref[:, :c0], preferred_element_type=F32)
    hg_ref[0] = jnp.dot(ub, w_ref[:, c0:c1], preferred_element_type=F32)
    gates_ref[0] = jnp.dot(ub, w_ref[:, c1:], preferred_element_type=F32)


def _mod_spec(tm, per_row):
    if per_row:
        return pl.BlockSpec((1, tm, D_MODEL), lambda g, i: (g, i, 0))
    return pl.BlockSpec((1, 1, D_MODEL), lambda g, i: (g, 0, 0))


def _input_proj(x, shift, scale, w_in_bf, tm, per_row):
    n_g, n_r, _ = x.shape

    def row_spec(width):
        return pl.BlockSpec((1, tm, width), lambda g, i: (g, i, 0))

    return pl.pallas_call(
        _proj_kernel,
        grid=(n_g, n_r // tm),
        in_specs=[row_spec(D_MODEL), _mod_spec(tm, per_row), _mod_spec(tm, per_row),
                  pl.BlockSpec((D_MODEL, N_IN), lambda g, i: (0, 0))],
        out_specs=[row_spec(S5_WIDTH), row_spec(4 * HG_WIDTH), row_spec(2 * D_MODEL)],
        out_shape=[jax.ShapeDtypeStruct((n_g, n_r, S5_WIDTH), F32),
                   jax.ShapeDtypeStruct((n_g, n_r, 4 * HG_WIDTH), F32),
                   jax.ShapeDtypeStruct((n_g, n_r, 2 * D_MODEL), F32)],
        compiler_params=_cparams("parallel", "parallel"),
        name="input_proj",
    )(x, shift, scale, w_in_bf)


S5_SLABS = S5_FLAT // LANES
S5_SCAN_SLABS = 4
MXU_TILE = 256


def _s5_kernel(u_ref, h0re_ref, h0im_ref, bb_ref, cc_ref, a_ref, d_ref,
               y_ref, hre_ref, him_ref, bu_scr, h_scr, *, tl, flat_rows):
    j = pl.program_id(1)
    rows = SUBLANES * tl
    u = u_ref[...] if flat_rows else u_ref[...].reshape(rows, S5_WIDTH)
    ub = u.astype(BF16)

    @pl.when(j == 0)
    def _():
        h_scr[0] = h0re_ref[...]
        h_scr[1] = h0im_ref[...]

    cols_per_in_tile = MXU_TILE * (S5_STATE // S5_GROUP)
    for n in range(2 * S5_FLAT // MXU_TILE):
        kt = (n * MXU_TILE % S5_FLAT) // cols_per_in_tile
        res = jnp.dot(ub[:, kt * MXU_TILE:(kt + 1) * MXU_TILE],
                      bb_ref[kt * MXU_TILE:(kt + 1) * MXU_TILE, n * MXU_TILE:(n + 1) * MXU_TILE],
                      preferred_element_type=F32)
        bu_scr[2 * n] = res[:, :LANES]
        bu_scr[2 * n + 1] = res[:, LANES:]

    for c0 in range(0, S5_SLABS, S5_SCAN_SLABS):
        slabs = range(c0, c0 + S5_SCAN_SLABS)
        a_re = [jnp.broadcast_to(a_ref[0:1, c * LANES:(c + 1) * LANES], (SUBLANES, LANES)) for c in slabs]
        a_im = [jnp.broadcast_to(a_ref[1:2, c * LANES:(c + 1) * LANES], (SUBLANES, LANES)) for c in slabs]

        def step(t, carry, slabs=slabs, a_re=a_re, a_im=a_im):
            rsel = pl.ds(t, SUBLANES, stride=tl)
            new = []
            for q, c in enumerate(slabs):
                h_re, h_im = carry[2 * q], carry[2 * q + 1]
                n_re = a_re[q] * h_re - a_im[q] * h_im + bu_scr[c, rsel, :]
                n_im = a_re[q] * h_im + a_im[q] * h_re + bu_scr[S5_SLABS + c, rsel, :]
                bu_scr[c, rsel, :] = n_re
                bu_scr[S5_SLABS + c, rsel, :] = n_im
                new += [n_re, n_im]
            return tuple(new)

        init = []
        for c in slabs:
            init += [h_scr[0, :, c * LANES:(c + 1) * LANES], h_scr[1, :, c * LANES:(c + 1) * LANES]]
        fin = lax.fori_loop(0, tl, step, tuple(init))
        for q, c in enumerate(slabs):
            h_scr[0, :, c * LANES:(c + 1) * LANES] = fin[2 * q]
            h_scr[1, :, c * LANES:(c + 1) * LANES] = fin[2 * q + 1]

    slabs_per_out = S5_SLABS * MXU_TILE // S5_WIDTH
    y_tiles = []
    for m in range(S5_WIDTH // MXU_TILE):
        acc = None
        for part in range(2):
            for c in range(m * slabs_per_out, (m + 1) * slabs_per_out, 2):
                s = part * S5_SLABS + c
                hb = jnp.concatenate([bu_scr[s], bu_scr[s + 1]], axis=1).astype(BF16)
                term = jnp.dot(hb, cc_ref[s * LANES:(s + 2) * LANES, m * MXU_TILE:(m + 1) * MXU_TILE],
                               preferred_element_type=F32)
                acc = term if acc is None else acc + term
        y_tiles.append(acc)
    y = jax.nn.gelu(jnp.concatenate(y_tiles, axis=1) + d_ref[...] * u)
    y_ref[...] = y if flat_rows else y.reshape(SUBLANES, tl, S5_WIDTH)
    hre_ref[...] = h_scr[0]
    him_ref[...] = h_scr[1]


def _s5_branch(u, h0_re, h0_im, bb, cc, a, d, tl):
    flat_rows = u.ndim == 2
    n_b = u.shape[0]
    n_t = 1 if flat_rows else u.shape[1] // tl
    if flat_rows:
        u_spec = pl.BlockSpec((SUBLANES, S5_WIDTH), lambda g, j: (g, 0))
    else:
        u_spec = pl.BlockSpec((SUBLANES, tl, S5_WIDTH), lambda g, j: (g, j, 0))
    st_spec = pl.BlockSpec((SUBLANES, S5_FLAT), lambda g, j: (g, 0))

    def const(shape):
        return pl.BlockSpec(shape, lambda g, j: (0, 0))

    return pl.pallas_call(
        functools.partial(_s5_kernel, tl=tl, flat_rows=flat_rows),
        grid=(n_b // SUBLANES, n_t),
        in_specs=[u_spec, st_spec, st_spec, const((S5_WIDTH, 2 * S5_FLAT)),
                  const((2 * S5_FLAT, S5_WIDTH)), const((2, S5_FLAT)), const((1, S5_WIDTH))],
        out_specs=[u_spec, st_spec, st_spec],
        out_shape=[jax.ShapeDtypeStruct(u.shape, F32),
                   jax.ShapeDtypeStruct((n_b, S5_FLAT), F32),
                   jax.ShapeDtypeStruct((n_b, S5_FLAT), F32)],
        scratch_shapes=[pltpu.VMEM((2 * S5_SLABS, SUBLANES * tl, LANES), F32),
                        pltpu.VMEM((2, SUBLANES, S5_FLAT), F32)],
        compiler_params=_cparams("parallel", "arbitrary"),
        name="s5_branch",
    )(u, h0_re, h0_im, bb, cc, a, d)


def _s5_params(lam_re, lam_im, log_dt, b_re, b_im, c_re, c_im, d_skip):
    dt = jnp.exp(log_dt)[:, None]
    mag = jnp.exp(lam_re * dt)
    ang = lam_im * dt
    ab_re, ab_im = mag * jnp.cos(ang), mag * jnp.sin(ang)
    den = jnp.square(lam_re) + jnp.square(lam_im)
    nr, ni = ab_re - 1.0, ab_im
    zf_re = (nr * lam_re + ni * lam_im) / den
    zf_im = (ni * lam_re - nr * lam_im) / den
    bb_re = zf_re[..., None] * b_re - zf_im[..., None] * b_im
    bb_im = zf_re[..., None] * b_im + zf_im[..., None] * b_re
    eye = jnp.eye(S5_GROUPS, dtype=F32)

    def in_blockdiag(m):
        return jnp.einsum('gph,gk->ghkp', m, eye).reshape(S5_WIDTH, S5_FLAT)

    def out_blockdiag(m):
        return jnp.einsum('ghp,gk->gpkh', m, eye).reshape(S5_FLAT, S5_WIDTH)

    bb = jnp.concatenate([in_blockdiag(bb_re), in_blockdiag(bb_im)], axis=1).astype(BF16)
    cc = jnp.concatenate([out_blockdiag(c_re), -out_blockdiag(c_im)], axis=0).astype(BF16)
    a = jnp.stack([ab_re.reshape(S5_FLAT), ab_im.reshape(S5_FLAT)])
    return bb, cc, a, d_skip.reshape(1, S5_WIDTH)


def _hgrn_kernel(hg_ref, s0_ref, lb_ref, gw_ref, y_ref, sout_ref, st_scr,
                 *, tl, valid_len, first_layer):
    j = pl.program_id(1)
    n_t = pl.num_programs(1)

    @pl.when(j == 0)
    def _():
        for h in range(HG_HEADS):
            st_scr[h] = s0_ref[0, h].T

    row = lax.broadcasted_iota(jnp.int32, (HG_CHUNK, HG_WIDTH), 0)
    row_h = lax.broadcasted_iota(jnp.int32, (HG_CHUNK, HG_HEAD_DIM), 0)
    lb = lb_ref[...]
    gw = gw_ref[...]

    def chunk_body(c, carry):
        r0 = pl.multiple_of(c * HG_CHUNK, HG_CHUNK)
        blk = hg_ref[0, pl.ds(r0, HG_CHUNK), :]
        q = blk[:, 0:HG_WIDTH]
        f_pre = blk[:, HG_WIDTH:2 * HG_WIDTH]
        v = blk[:, 2 * HG_WIDTH:3 * HG_WIDTH]
        g_out = blk[:, 3 * HG_WIDTH:4 * HG_WIDTH]
        if first_layer:
            log_f = jnp.minimum(f_pre, 0.0) - jnp.log1p(jnp.exp(-jnp.abs(f_pre)))
            k = jax.nn.sigmoid(-f_pre)
        else:
            log_f = jnp.log(lb + (1.0 - lb) * jax.nn.sigmoid(f_pre))
            k = (1.0 - lb) * jax.nn.sigmoid(-f_pre)
        if valid_len is not None:
            live = (j * tl + r0 + row) < valid_len
            log_f = jnp.where(live, log_f, 0.0)
            k = jnp.where(live, k, 0.0)
        b = log_f
        for sh in (1, 2, 4, 8):
            b = b + jnp.where(row >= sh, pltpu.roll(b, sh, 0), 0.0)

        for h in range(HG_HEADS):
            hs = slice(h * HG_HEAD_DIM, (h + 1) * HG_HEAD_DIM)
            qh, kh, vh, bh = q[:, hs], k[:, hs], v[:, hs], b[:, hs]
            b_last = bh[HG_CHUNK - 1:HG_CHUNK, :]
            st = st_scr[h]
            o = lax.dot_general((qh * jnp.exp(bh)).astype(BF16), st.astype(BF16),
                                (((1,), (1,)), ((), ())), preferred_element_type=F32)
            for s in range(HG_CHUNK):
                dec = jnp.exp(jnp.minimum(bh - bh[s:s + 1, :], 0.0))
                att = jnp.sum(qh * kh[s:s + 1, :] * dec, axis=-1, keepdims=True)
                att = jnp.where(row_h >= s, att, 0.0)
                o = o + att * vh[s:s + 1, :]
            k_dec = kh * jnp.exp(b_last - bh)
            upd = lax.dot_general(vh.astype(BF16), k_dec.astype(BF16),
                                  (((0,), (0,)), ((), ())), preferred_element_type=F32)
            st_scr[h] = st * jnp.exp(b_last) + upd
            o = o * lax.rsqrt(jnp.mean(o * o, axis=-1, keepdims=True) + RMS_EPS) * gw
            gh = g_out[:, hs]
            y_ref[0, pl.ds(r0, HG_CHUNK), hs] = o * (gh * jax.nn.sigmoid(gh))
        return carry

    lax.fori_loop(0, tl // HG_CHUNK, chunk_body, 0)

    @pl.when(j == n_t - 1)
    def _():
        for h in range(HG_HEADS):
            sout_ref[0, h] = st_scr[h].T


def _hgrn_branch(hg, s0, lb, gnorm_w, tl, valid_len, first_layer):
    n_b, n_l, _ = hg.shape
    st_spec = pl.BlockSpec((1, HG_HEADS, HG_HEAD_DIM, HG_HEAD_DIM), lambda b, j: (b, 0, 0, 0))
    return pl.pallas_call(
        functools.partial(_hgrn_kernel, tl=tl, valid_len=valid_len, first_layer=first_layer),
        grid=(n_b, n_l // tl),
        in_specs=[pl.BlockSpec((1, tl, 4 * HG_WIDTH), lambda b, j: (b, j, 0)),
                  st_spec,
                  pl.BlockSpec((1, HG_WIDTH), lambda b, j: (0, 0)),
                  pl.BlockSpec((1, HG_HEAD_DIM), lambda b, j: (0, 0))],
        out_specs=[pl.BlockSpec((1, tl, HG_WIDTH), lambda b, j: (b, j, 0)), st_spec],
        out_shape=[jax.ShapeDtypeStruct((n_b, n_l, HG_WIDTH), F32),
                   jax.ShapeDtypeStruct(s0.shape, F32)],
        scratch_shapes=[pltpu.VMEM((HG_HEADS, HG_HEAD_DIM, HG_HEAD_DIM), F32)],
        compiler_params=_cparams("parallel", "arbitrary"),
        name="hgrn_branch",
    )(hg, s0, lb.reshape(1, HG_WIDTH), gnorm_w.reshape(1, HG_HEAD_DIM))


def _mix_kernel(ys5_ref, yhg_ref, gates_ref, x_ref, gate1_ref, shift2_ref, scale2_ref,
                wglu_ref, bglu_ref, wus_ref, wuh_ref, wout_ref, g_ref, b_ref, rw_ref, rb_ref,
                x1_ref, u2_ref, topi_ref, topw_ref):
    ys = ys5_ref[0]
    glu = ys * jax.nn.sigmoid(
        jnp.dot(ys.astype(BF16), wglu_ref[...], preferred_element_type=F32) + bglu_ref[...])
    gates = jax.nn.sigmoid(gates_ref[0])
    merged = (gates[:, :D_MODEL] * jnp.dot(glu.astype(BF16), wus_ref[...], preferred_element_type=F32)
              + gates[:, D_MODEL:] * jnp.dot(yhg_ref[0].astype(BF16), wuh_ref[...],
                                             preferred_element_type=F32))
    mix = jnp.dot(merged.astype(BF16), wout_ref[...], preferred_element_type=F32)
    x1 = _layer_norm(DEEPNORM_ALPHA * x_ref[0] + gate1_ref[0] * mix, g_ref[...], b_ref[...])
    x1_ref[0] = x1
    u2 = x1 * (1.0 + scale2_ref[0]) + shift2_ref[0]
    u2_ref[0] = u2
    logits = jnp.dot(u2, rw_ref[...], preferred_element_type=F32,
                     precision=lax.Precision.HIGHEST) + rb_ref[...]
    lane = lax.broadcasted_iota(jnp.int32, logits.shape, 1)
    lane_f = lane.astype(F32)
    vals, idxs = [], []
    for _ in range(TOP_K):
        m = jnp.max(logits, axis=-1, keepdims=True)
        idx = jnp.min(jnp.where(logits == m, lane_f, float(LANES)), axis=-1,
                      keepdims=True).astype(jnp.int32)
        vals.append(m)
        idxs.append(idx)
        logits = jnp.where(lane == idx, NEG_BIG, logits)
    exps = [jnp.exp(val - vals[0]) for val in vals]
    denom = exps[0] + exps[1] + exps[2] + exps[3]
    topi = jnp.zeros(lane.shape, jnp.int32)
    topw = jnp.zeros(lane.shape, F32)
    for kk in range(TOP_K):
        topi = jnp.where(lane == kk, idxs[kk], topi)
        topw = jnp.where(lane == kk, exps[kk] / denom, topw)
    topi_ref[0] = topi
    topw_ref[0] = topw


def _mix(ys5, yhg, gates, x, gate1, shift2, scale2, lw, tm, per_row):
    n_g, n_r, _ = x.shape

    def row_spec(width):
        return pl.BlockSpec((1, tm, width), lambda g, i: (g, i, 0))

    def const(shape):
        return pl.BlockSpec(shape, lambda g, i: (0, 0))

    ms = _mod_spec(tm, per_row)
    return pl.pallas_call(
        _mix_kernel,
        grid=(n_g, n_r // tm),
        in_specs=[row_spec(S5_WIDTH), row_spec(HG_WIDTH), row_spec(2 * D_MODEL), row_spec(D_MODEL),
                  ms, ms, ms,
                  const((S5_WIDTH, S5_WIDTH)), const((1, S5_WIDTH)),
                  const((S5_WIDTH, D_MODEL)), const((HG_WIDTH, D_MODEL)),
                  const((D_MODEL, D_MODEL)), const((1, D_MODEL)), const((1, D_MODEL)),
                  const((D_MODEL, LANES)), const((1, LANES))],
        out_specs=[row_spec(D_MODEL), row_spec(D_MODEL), row_spec(LANES), row_spec(LANES)],
        out_shape=[jax.ShapeDtypeStruct((n_g, n_r, D_MODEL), F32),
                   jax.ShapeDtypeStruct((n_g, n_r, D_MODEL), F32),
                   jax.ShapeDtypeStruct((n_g, n_r, LANES), jnp.int32),
                   jax.ShapeDtypeStruct((n_g, n_r, LANES), F32)],
        compiler_params=_cparams("parallel", "parallel"),
        name="branch_mix",
    )(ys5, yhg, gates, x, gate1, shift2, scale2,
      lw['w_glu'], lw['b_glu'], lw['w_up_s5'], lw['w_up_hg'], lw['w_out'],
      lw['ln1_g'], lw['ln1_b'], lw['router_w'], lw['router_b'])


def _moe_kernel(block_e_ref, slot_tok_ref, n_used_ref, u_hbm, sw_ref, wgu_ref, bgu_ref,
                wd_ref, bd_ref, y_ref, xbuf, wgu_bf, wd_bf, sem):
    i = pl.program_id(0)
    n_used = n_used_ref[0]
    cur = i % 2

    def gather(blk, buf):
        def issue(r, carry):
            tok = slot_tok_ref[blk * MOE_BLOCK + r]
            pltpu.make_async_copy(u_hbm.at[pl.ds(tok, 1)], xbuf.at[buf, pl.ds(r, 1)],
                                  sem.at[buf]).start()
            return carry
        lax.fori_loop(0, MOE_BLOCK, issue, 0)

    @pl.when(jnp.logical_and(i == 0, n_used > 0))
    def _():
        gather(0, 0)

    @pl.when(i + 1 < n_used)
    def _():
        gather(i + 1, 1 - cur)

    e = block_e_ref[i]
    e_prev = block_e_ref[jnp.maximum(i - 1, 0)]

    @pl.when(jnp.logical_and(i < n_used, jnp.logical_or(i == 0, e != e_prev)))
    def _():
        wgu_bf[...] = wgu_ref[0].astype(BF16)
        wd_bf[...] = wd_ref[0].astype(BF16)

    @pl.when(i < n_used)
    def _():
        pltpu.make_async_copy(u_hbm.at[pl.ds(0, MOE_BLOCK)], xbuf.at[cur], sem.at[cur]).wait()
        x = xbuf[cur]
        h = jnp.dot(x.astype(BF16), wgu_bf[...], preferred_element_type=F32) + bgu_ref[0]
        gate = jnp.minimum(h[:, :D_EXPERT], SWIGLU_LIMIT)
        up = jnp.clip(h[:, D_EXPERT:], -SWIGLU_LIMIT, SWIGLU_LIMIT)
        act = gate * jax.nn.sigmoid(SWIGLU_ALPHA * gate) * (up + 1.0)
        y = jnp.dot(act.astype(BF16), wd_bf[...], preferred_element_type=F32) + bd_ref[0]
        y_ref[...] = y * sw_ref[...]

    @pl.when(i >= n_used)
    def _():
        y_ref[...] = jnp.zeros(y_ref.shape, F32)


def _moe_experts(u_all, block_e, slot_tok, n_used, slot_w, w_gate_up, b_gate_up, w_down, b_down):
    n_blocks = block_e.shape[0]

    def expert_spec(shape):
        return pl.BlockSpec((1,) + shape, lambda i, be, st, nu: (be[i], 0, 0))

    grid_spec = pltpu.PrefetchScalarGridSpec(
        num_scalar_prefetch=3,
        grid=(n_blocks,),
        in_specs=[pl.BlockSpec(memory_space=pl.ANY),
                  pl.BlockSpec((MOE_BLOCK, 1), lambda i, be, st, nu: (i, 0)),
                  expert_spec((D_MODEL, 2 * D_EXPERT)), expert_spec((1, 2 * D_EXPERT)),
                  expert_spec((D_EXPERT, D_MODEL)), expert_spec((1, D_MODEL))],
        out_specs=pl.BlockSpec((MOE_BLOCK, D_MODEL), lambda i, be, st, nu: (i, 0)),
        scratch_shapes=[pltpu.VMEM((2, MOE_BLOCK, D_MODEL), F32),
                        pltpu.VMEM((D_MODEL, 2 * D_EXPERT), BF16),
                        pltpu.VMEM((D_EXPERT, D_MODEL), BF16),
                        pltpu.SemaphoreType.DMA((2,))],
    )
    return pl.pallas_call(
        _moe_kernel,
        grid_spec=grid_spec,
        out_shape=jax.ShapeDtypeStruct((n_blocks * MOE_BLOCK, D_MODEL), F32),
        compiler_params=_cparams("arbitrary"),
        name="moe_experts",
    )(block_e, slot_tok, n_used, u_all, slot_w,
      w_gate_up, b_gate_up.reshape(N_EXPERTS, 1, 2 * D_EXPERT),
      w_down, b_down.reshape(N_EXPERTS, 1, D_MODEL))


def _combine_kernel(dest_ref, yb_hbm, x1_ref, gate2_ref, g_ref, b_ref, o_ref, ybuf, sem,
                    *, tm, tok_offset):
    g = pl.program_id(0)
    i = pl.program_id(1)
    base = (tok_offset + (g * pl.num_programs(1) + i) * tm) * TOP_K

    def issue(r, carry):
        for kk in range(TOP_K):
            slot = dest_ref[base + r * TOP_K + kk]
            pltpu.make_async_copy(yb_hbm.at[pl.ds(slot, 1)], ybuf.at[kk, pl.ds(r, 1)], sem).start()
        return carry

    lax.fori_loop(0, tm, issue, 0)
    for kk in range(TOP_K):
        pltpu.make_async_copy(yb_hbm.at[pl.ds(0, tm)], ybuf.at[kk], sem).wait()
    ffn = (ybuf[0] + ybuf[1]) + (ybuf[2] + ybuf[3])
    o_ref[0] = _layer_norm(DEEPNORM_ALPHA * x1_ref[0] + gate2_ref[0] * ffn, g_ref[...], b_ref[...])


def _combine(dest, yb, x1, gate2, ln_g, ln_b, tm, per_row, tok_offset):
    n_g, n_r, _ = x1.shape
    ms = (pl.BlockSpec((1, tm, D_MODEL), lambda g, i, d: (g, i, 0)) if per_row
          else pl.BlockSpec((1, 1, D_MODEL), lambda g, i, d: (g, 0, 0)))
    grid_spec = pltpu.PrefetchScalarGridSpec(
        num_scalar_prefetch=1,
        grid=(n_g, n_r // tm),
        in_specs=[pl.BlockSpec(memory_space=pl.ANY),
                  pl.BlockSpec((1, tm, D_MODEL), lambda g, i, d: (g, i, 0)),
                  ms,
                  pl.BlockSpec((1, D_MODEL), lambda g, i, d: (0, 0)),
                  pl.BlockSpec((1, D_MODEL), lambda g, i, d: (0, 0))],
        out_specs=pl.BlockSpec((1, tm, D_MODEL), lambda g, i, d: (g, i, 0)),
        scratch_shapes=[pltpu.VMEM((TOP_K, tm, D_MODEL), F32), pltpu.SemaphoreType.DMA(())],
    )
    return pl.pallas_call(
        functools.partial(_combine_kernel, tm=tm, tok_offset=tok_offset),
        grid_spec=grid_spec,
        out_shape=jax.ShapeDtypeStruct(x1.shape, F32),
        compiler_params=_cparams("arbitrary", "arbitrary"),
        name="moe_combine",
    )(dest, yb, x1, gate2, ln_g, ln_b)


def _routing_tables(top_idx, top_w):
    n_tok = top_idx.shape[0]
    n_assign = n_tok * TOP_K
    flat_e = top_idx.reshape(-1)
    onehot = (flat_e[:, None] == jnp.arange(N_EXPERTS, dtype=jnp.int32)[None, :]).astype(jnp.int32)
    counts = jnp.sum(onehot, axis=0)
    rank = jnp.sum((jnp.cumsum(onehot, axis=0) - onehot) * onehot, axis=1)
    padded = (counts + MOE_BLOCK - 1) // MOE_BLOCK * MOE_BLOCK
    pad_end = jnp.cumsum(padded)
    pad_start = pad_end - padded
    dest = (pad_start[flat_e] + rank).astype(jnp.int32)
    n_blocks = -(-n_assign // MOE_BLOCK) + N_EXPERTS
    flat_t = jnp.repeat(jnp.arange(n_tok, dtype=jnp.int32), TOP_K)
    slot_tok = jnp.zeros((n_blocks * MOE_BLOCK,), jnp.int32).at[dest].set(flat_t)
    slot_w = jnp.zeros((n_blocks * MOE_BLOCK,), F32).at[dest].set(top_w.reshape(-1))
    block_e = jnp.minimum(
        jnp.searchsorted(pad_end, jnp.arange(n_blocks, dtype=jnp.int32) * MOE_BLOCK, side='right'),
        N_EXPERTS - 1).astype(jnp.int32)
    n_used = (pad_end[-1] // MOE_BLOCK).astype(jnp.int32).reshape(1)
    return dest, slot_tok, slot_w.reshape(-1, 1), block_e, n_used


def kernel(x_prompt, x_sample, state_s5_re, state_s5_im, state_hgrn, c_prompt, c_sample,
           w_ada, b_ada, w_in, s5_lambda_re, s5_lambda_im, s5_log_dt, s5_b_re, s5_b_im,
           s5_c_re, s5_c_im, s5_d, w_glu, b_glu, w_up_s5, hgrn_lb_logits, hgrn_gnorm_w,
           w_up_hg, w_out, ln1_g, ln1_b, router_w, router_b, w_gate_up, b_gate_up,
           w_down, b_down, ln2_g, ln2_b):
    n_p, seq, _ = x_prompt.shape
    n_s = x_sample.shape[0]
    n_tok_p = n_p * seq

    lb_p = jax.nn.softmax(hgrn_lb_logits.astype(F32), axis=0)
    lb_all = jnp.cumsum(lb_p, axis=0) - lb_p[0]

    mod = _modulation(jnp.concatenate([c_prompt, c_sample], axis=0), w_ada, b_ada)

    def mods(layer, which):
        sl = slice(which * D_MODEL, (which + 1) * D_MODEL)
        return (mod[layer, :n_p, sl].reshape(n_p, 1, D_MODEL),
                mod[layer, n_p:, sl].reshape(1, n_s, D_MODEL))

    xp = x_prompt
    xs = x_sample.reshape(1, n_s, D_MODEL)
    zero_s5 = jnp.zeros((n_p, S5_FLAT), F32)
    zero_hg = jnp.zeros((n_p, HG_HEADS, HG_HEAD_DIM, HG_HEAD_DIM), F32)
    outs = {k: [] for k in ('p_re', 'p_im', 'p_hg', 's_re', 's_im', 's_hg')}

    for l in range(DEPTH):
        shift1, scale1, gate1, shift2, scale2, gate2 = [mods(l, w) for w in range(6)]
        w_in_bf = w_in[l].astype(BF16)
        bb, cc, a, d = _s5_params(s5_lambda_re[l], s5_lambda_im[l], s5_log_dt[l], s5_b_re[l],
                                  s5_b_im[l], s5_c_re[l], s5_c_im[l], s5_d[l])
        router_w_pad = jnp.zeros((D_MODEL, LANES), F32).at[:, :N_EXPERTS].set(router_w[l])
        router_b_pad = jnp.full((1, LANES), NEG_BIG, F32).at[0, :N_EXPERTS].set(router_b[l])
        lw = {'w_glu': w_glu[l].astype(BF16), 'b_glu': b_glu[l].reshape(1, S5_WIDTH),
              'w_up_s5': w_up_s5[l].astype(BF16), 'w_up_hg': w_up_hg[l].astype(BF16),
              'w_out': w_out[l].astype(BF16), 'ln1_g': ln1_g[l].reshape(1, D_MODEL),
              'ln1_b': ln1_b[l].reshape(1, D_MODEL), 'router_w': router_w_pad,
              'router_b': router_b_pad}
        first = l == 0

        us5_p, hg_p, gates_p = _input_proj(xp, shift1[0], scale1[0], w_in_bf, 256, False)
        ys5_p, pre, pim = _s5_branch(us5_p, zero_s5, zero_s5, bb, cc, a, d, 64)
        yhg_p, phg = _hgrn_branch(hg_p, zero_hg, lb_all[l], hgrn_gnorm_w[l], 128, None, first)
        x1_p, u2_p, ti_p, tw_p = _mix(ys5_p, yhg_p, gates_p, xp, gate1[0], shift2[0], scale2[0],
                                      lw, 256, False)
        us5_s, hg_s, gates_s = _input_proj(xs, shift1[1], scale1[1], w_in_bf, n_s, True)
        ys5_s, sre, sim = _s5_branch(us5_s.reshape(n_s, S5_WIDTH),
                                     state_s5_re[l].reshape(n_s, S5_FLAT),
                                     state_s5_im[l].reshape(n_s, S5_FLAT), bb, cc, a, d, 1)
        hg_s_pad = jnp.pad(hg_s.reshape(n_s, 1, 4 * HG_WIDTH), ((0, 0), (0, HG_CHUNK - 1), (0, 0)))
        yhg_s, shg = _hgrn_branch(hg_s_pad, state_hgrn[l], lb_all[l], hgrn_gnorm_w[l],
                                  HG_CHUNK, 1, first)
        x1_s, u2_s, ti_s, tw_s = _mix(ys5_s.reshape(1, n_s, S5_WIDTH),
                                      yhg_s[:, 0, :].reshape(1, n_s, HG_WIDTH), gates_s, xs,
                                      gate1[1], shift2[1], scale2[1], lw, n_s, True)

        u_all = jnp.concatenate([u2_p.reshape(n_tok_p, D_MODEL), u2_s.reshape(n_s, D_MODEL)], axis=0)
        top_idx = jnp.concatenate([ti_p.reshape(n_tok_p, LANES)[:, :TOP_K],
                                   ti_s.reshape(n_s, LANES)[:, :TOP_K]], axis=0)
        top_w = jnp.concatenate([tw_p.reshape(n_tok_p, LANES)[:, :TOP_K],
                                 tw_s.reshape(n_s, LANES)[:, :TOP_K]], axis=0)
        dest, slot_tok, slot_w, block_e, n_used = _routing_tables(top_idx, top_w)
        yb = _moe_experts(u_all, block_e, slot_tok, n_used, slot_w,
                          w_gate_up[l], b_gate_up[l], w_down[l], b_down[l])
        g2, b2 = ln2_g[l].reshape(1, D_MODEL), ln2_b[l].reshape(1, D_MODEL)
        xp = _combine(dest, yb, x1_p, gate2[0], g2, b2, 256, False, 0)
        xs = _combine(dest, yb, x1_s, gate2[1], g2, b2, n_s, True, n_tok_p)

        outs['p_re'].append(pre.reshape(n_p, S5_GROUPS, S5_STATE))
        outs['p_im'].append(pim.reshape(n_p, S5_GROUPS, S5_STATE))
        outs['p_hg'].append(phg)
        outs['s_re'].append(sre.reshape(n_s, S5_GROUPS, S5_STATE))
        outs['s_im'].append(sim.reshape(n_s, S5_GROUPS, S5_STATE))
        outs['s_hg'].append(shg)

    return (xp, xs.reshape(n_s, 1, D_MODEL),
            jnp.stack(outs['p_re']), jnp.stack(outs['p_im']), jnp.stack(outs['p_hg']),
            jnp.stack(outs['s_re']), jnp.stack(outs['s_im']), jnp.stack(outs['s_hg']))
```

```python
import functools

import jax
import jax.numpy as jnp
from jax import lax
from jax.experimental import pallas as pl
from jax.experimental.pallas import tpu as pltpu

F32 = jnp.float32
BF16 = jnp.bfloat16

D_MODEL = 1024
DEPTH = 2
S5_WIDTH = 512
S5_GROUP = 16
S5_GROUPS = 32
S5_STATE = 64
S5_FLAT = S5_GROUPS * S5_STATE
HG_WIDTH = 512
HG_HEAD_DIM = 128
HG_HEADS = 4
HG_CHUNK = 16
RMS_EPS = 1e-6
N_IN = S5_WIDTH + 4 * HG_WIDTH + 2 * D_MODEL
N_EXPERTS = 32
TOP_K = 4
D_EXPERT = D_MODEL
SWIGLU_LIMIT = 7.0
SWIGLU_ALPHA = 1.702
MOE_BLOCK = 128
DEEPNORM_ALPHA = (2 * DEPTH) ** 0.25
LN_EPS = 1e-5

LANES = 128
SUBLANES = 8
MXU_TILE = 256
VMEM_LIMIT = 56 * 1024 * 1024
NEG_BIG = -1e30


def _cparams(*sem):
    return pltpu.CompilerParams(dimension_semantics=sem, vmem_limit_bytes=VMEM_LIMIT)


def _layer_norm(x, g, b):
    mu = jnp.mean(x, axis=-1, keepdims=True)
    xc = x - mu
    var = jnp.mean(xc * xc, axis=-1, keepdims=True)
    return xc * lax.rsqrt(var + LN_EPS) * g + b


def _mod_kernel(c_ref, w_ref, b_ref, o_ref):
    c = c_ref[...]
    s = c * jax.nn.sigmoid(c)
    o_ref[0] = jnp.dot(s.astype(BF16), w_ref[0].astype(BF16),
                       preferred_element_type=F32) + b_ref[0]


def _modulation(c_all, w_ada, b_ada):
    n_rows = c_all.shape[0]
    tn = 1536
    return pl.pallas_call(
        _mod_kernel,
        grid=(DEPTH, 6 * D_MODEL // tn),
        in_specs=[
            pl.BlockSpec((n_rows, D_MODEL), lambda l, j: (0, 0)),
            pl.BlockSpec((1, D_MODEL, tn), lambda l, j: (l, 0, j)),
            pl.BlockSpec((1, 1, tn), lambda l, j: (l, 0, j)),
        ],
        out_specs=pl.BlockSpec((1, n_rows, tn), lambda l, j: (l, 0, j)),
        out_shape=jax.ShapeDtypeStruct((DEPTH, n_rows, 6 * D_MODEL), F32),
        compiler_params=_cparams("parallel", "parallel"),
        name="adaln_mod",
    )(c_all, w_ada, b_ada.reshape(DEPTH, 1, 6 * D_MODEL))


def _proj_kernel(x_ref, shift_ref, scale_ref, w_ref, us5_ref, hg_ref, gates_ref):
    u = x_ref[0] * (1.0 + scale_ref[0]) + shift_ref[0]
    ub = u.astype(BF16)
    c0, c1 = S5_WIDTH, S5_WIDTH + 4 * HG_WIDTH
    us5_ref[0] = jnp.dot(ub, w_ref[:, :c0], preferred_element_type=F32)
    hg_ref[0] = jnp.dot(ub, w_ref[:, c0:c1], preferred_element_type=F32)
    gates_ref[0] = jnp.dot(ub, w_ref[:, c1:], preferred_element_type=F32)


def _mod_spec(tm, per_row):
    if per_row:
        return pl.BlockSpec((1, tm, D_MODEL), lambda g, i: (g, i, 0))
    return pl.BlockSpec((1, 1, D_MODEL), lambda g, i: (g, 0, 0))


def _input_proj(x, shift, scale, w_in_bf, tm, per_row):
    n_g, n_r, _ = x.shape

    def row_spec(width):
        return pl.BlockSpec((1, tm, width), lambda g, i: (g, i, 0))

    return pl.pallas_call(
        _proj_kernel,
        grid=(n_g, n_r // tm),
        in_specs=[row_spec(D_MODEL), _mod_spec(tm, per_row), _mod_spec(tm, per_row),
                  pl.BlockSpec((D_MODEL, N_IN), lambda g, i: (0, 0))],
        out_specs=[row_spec(S5_WIDTH), row_spec(4 * HG_WIDTH), row_spec(2 * D_MODEL)],
        out_shape=[jax.ShapeDtypeStruct((n_g, n_r, S5_WIDTH), F32),
                   jax.ShapeDtypeStruct((n_g, n_r, 4 * HG_WIDTH), F32),
                   jax.ShapeDtypeStruct((n_g, n_r, 2 * D_MODEL), F32)],
        compiler_params=_cparams("parallel", "parallel"),
        name="input_proj",
    )(x, shift, scale, w_in_bf)


S5_SLABS = S5_FLAT // LANES
S5_SCAN_SLABS = 8
S5_SCAN_UNROLL = 4


def _split_bf16x3(x):
    p1 = x.astype(BF16)
    r1 = x - p1.astype(F32)
    p2 = r1.astype(BF16)
    p3 = (r1 - p2.astype(F32)).astype(BF16)
    return jnp.concatenate([p1, p2, p3], axis=1)


def _s5_kernel(u_ref, h0re_ref, h0im_ref, bb_ref, cc_ref, a_ref, d_ref, perm_ref, perm_t_ref,
               y_ref, hre_ref, him_ref, bu_scr, hs_scr, h_scr, *, tl):
    j = pl.program_id(1)
    rows = SUBLANES * tl
    u = u_ref[...].reshape(rows, S5_WIDTH)
    ub = u.astype(BF16)
    if tl > 1:
        ub = jnp.dot(perm_ref[...], ub, preferred_element_type=F32).astype(BF16)

    @pl.when(j == 0)
    def _():
        h_scr[0] = h0re_ref[...]
        h_scr[1] = h0im_ref[...]

    cols_per_in_tile = MXU_TILE * (S5_STATE // S5_GROUP)
    for n in range(2 * S5_FLAT // MXU_TILE):
        kt = (n * MXU_TILE % S5_FLAT) // cols_per_in_tile
        res = jnp.dot(ub[:, kt * MXU_TILE:(kt + 1) * MXU_TILE],
                      bb_ref[kt * MXU_TILE:(kt + 1) * MXU_TILE, n * MXU_TILE:(n + 1) * MXU_TILE],
                      preferred_element_type=F32)
        bu_scr[2 * n] = res[:, :LANES]
        bu_scr[2 * n + 1] = res[:, LANES:]

    for c0 in range(0, S5_SLABS, S5_SCAN_SLABS):
        slabs = range(c0, c0 + S5_SCAN_SLABS)
        a_re = [jnp.broadcast_to(a_ref[0:1, c * LANES:(c + 1) * LANES], (SUBLANES, LANES)) for c in slabs]
        a_im = [jnp.broadcast_to(a_ref[1:2, c * LANES:(c + 1) * LANES], (SUBLANES, LANES)) for c in slabs]

        def step(t, carry, slabs=slabs, a_re=a_re, a_im=a_im):
            rsel = pl.ds(pl.multiple_of(t * SUBLANES, SUBLANES), SUBLANES)
            new = []
            for q, c in enumerate(slabs):
                h_re, h_im = carry[2 * q], carry[2 * q + 1]
                n_re = a_re[q] * h_re - a_im[q] * h_im + bu_scr[c, rsel, :]
                n_im = a_re[q] * h_im + a_im[q] * h_re + bu_scr[S5_SLABS + c, rsel, :]
                hs_scr[c, rsel, :] = n_re
                hs_scr[S5_SLABS + c, rsel, :] = n_im
                new += [n_re, n_im]
            return tuple(new)

        init = []
        for c in slabs:
            init += [h_scr[0, :, c * LANES:(c + 1) * LANES], h_scr[1, :, c * LANES:(c + 1) * LANES]]
        fin = lax.fori_loop(0, tl, step, tuple(init), unroll=min(S5_SCAN_UNROLL, tl))
        for q, c in enumerate(slabs):
            h_scr[0, :, c * LANES:(c + 1) * LANES] = fin[2 * q]
            h_scr[1, :, c * LANES:(c + 1) * LANES] = fin[2 * q + 1]

    slabs_per_out = S5_SLABS * MXU_TILE // S5_WIDTH
    y_tiles = []
    for m in range(S5_WIDTH // MXU_TILE):
        acc = None
        for part in range(2):
            for c in range(m * slabs_per_out, (m + 1) * slabs_per_out, 2):
                s = part * S5_SLABS + c
                hb = jnp.concatenate([hs_scr[s], hs_scr[s + 1]], axis=1).astype(BF16)
                term = jnp.dot(hb, cc_ref[s * LANES:(s + 2) * LANES, m * MXU_TILE:(m + 1) * MXU_TILE],
                               preferred_element_type=F32)
                acc = term if acc is None else acc + term
        y_tiles.append(acc)
    ch = jnp.concatenate(y_tiles, axis=1)
    if tl > 1:
        back = jnp.dot(perm_t_ref[...], _split_bf16x3(ch), preferred_element_type=F32)
        ch = back[:, :S5_WIDTH] + back[:, S5_WIDTH:2 * S5_WIDTH] + back[:, 2 * S5_WIDTH:]
    y = jax.nn.gelu(ch + d_ref[...] * u)
    y_ref[...] = y.reshape(y_ref.shape)
    hre_ref[...] = h_scr[0]
    him_ref[...] = h_scr[1]


def _s5_branch(u, h0_re, h0_im, bb, cc, a, d, tl):
    n_b, n_l, _ = u.shape
    rows = SUBLANES * tl
    if n_l == 1:
        u = u.reshape(n_b // SUBLANES, SUBLANES, S5_WIDTH)
        u_spec = pl.BlockSpec((None, SUBLANES, S5_WIDTH), lambda g, j: (g, 0, 0))
    else:
        u_spec = pl.BlockSpec((SUBLANES, tl, S5_WIDTH), lambda g, j: (g, j, 0))
    st_spec = pl.BlockSpec((SUBLANES, S5_FLAT), lambda g, j: (g, 0))

    def const(shape):
        return pl.BlockSpec(shape, lambda g, j: (0, 0))

    src = (jnp.arange(rows) % SUBLANES) * tl + jnp.arange(rows) // SUBLANES
    perm = (src[:, None] == jnp.arange(rows)[None, :]).astype(BF16)
    slab_scratch = pltpu.VMEM((2 * S5_SLABS, rows, LANES), F32)
    return pl.pallas_call(
        functools.partial(_s5_kernel, tl=tl),
        grid=(n_b // SUBLANES, n_l // tl),
        in_specs=[u_spec, st_spec, st_spec, const((S5_WIDTH, 2 * S5_FLAT)),
                  const((2 * S5_FLAT, S5_WIDTH)), const((2, S5_FLAT)), const((1, S5_WIDTH)),
                  const((rows, rows)), const((rows, rows))],
        out_specs=[u_spec, st_spec, st_spec],
        out_shape=[jax.ShapeDtypeStruct(u.shape, F32),
                   jax.ShapeDtypeStruct((n_b, S5_FLAT), F32),
                   jax.ShapeDtypeStruct((n_b, S5_FLAT), F32)],
        scratch_shapes=[slab_scratch, slab_scratch, pltpu.VMEM((2, SUBLANES, S5_FLAT), F32)],
        compiler_params=_cparams("parallel", "arbitrary"),
        name="s5_branch",
    )(u, h0_re, h0_im, bb, cc, a, d, perm, perm.T)


def _s5_params(lam_re, lam_im, log_dt, b_re, b_im, c_re, c_im, d_skip):
    dt = jnp.exp(log_dt)[:, None]
    mag = jnp.exp(lam_re * dt)
    ang = lam_im * dt
    ab_re, ab_im = mag * jnp.cos(ang), mag * jnp.sin(ang)
    den = jnp.square(lam_re) + jnp.square(lam_im)
    nr, ni = ab_re - 1.0, ab_im
    zf_re = (nr * lam_re + ni * lam_im) / den
    zf_im = (ni * lam_re - nr * lam_im) / den
    bb_re = zf_re[..., None] * b_re - zf_im[..., None] * b_im
    bb_im = zf_re[..., None] * b_im + zf_im[..., None] * b_re
    eye = jnp.eye(S5_GROUPS, dtype=F32)

    def in_blockdiag(m):
        return jnp.einsum('gph,gk->ghkp', m, eye).reshape(S5_WIDTH, S5_FLAT)

    def out_blockdiag(m):
        return jnp.einsum('ghp,gk->gpkh', m, eye).reshape(S5_FLAT, S5_WIDTH)

    bb = jnp.concatenate([in_blockdiag(bb_re), in_blockdiag(bb_im)], axis=1).astype(BF16)
    cc = jnp.concatenate([out_blockdiag(c_re), -out_blockdiag(c_im)], axis=0).astype(BF16)
    a = jnp.stack([ab_re.reshape(S5_FLAT), ab_im.reshape(S5_FLAT)])
    return bb, cc, a, d_skip.reshape(1, S5_WIDTH)


def _hgrn_kernel(hg_ref, s0_ref, lb_ref, gw_ref, y_ref, sout_ref, st_scr,
                 *, tl, valid_len, first_layer):
    j = pl.program_id(1)
    n_t = pl.num_programs(1)

    @pl.when(j == 0)
    def _():
        for h in range(HG_HEADS):
            st_scr[h] = s0_ref[0, 0, h].T

    row = lax.broadcasted_iota(jnp.int32, (HG_CHUNK, HG_WIDTH), 0)
    row_col = lax.broadcasted_iota(jnp.int32, (HG_CHUNK, 1), 0)
    lb = lb_ref[...]
    gw = gw_ref[...]

    def chunk_body(c, carry):
        r0 = pl.multiple_of(c * HG_CHUNK, HG_CHUNK)
        blk = hg_ref[0, pl.ds(r0, HG_CHUNK), :]
        q = blk[:, 0:HG_WIDTH]
        f_pre = blk[:, HG_WIDTH:2 * HG_WIDTH]
        v = blk[:, 2 * HG_WIDTH:3 * HG_WIDTH]
        g_out = blk[:, 3 * HG_WIDTH:4 * HG_WIDTH]
        if first_layer:
            log_f = jnp.minimum(f_pre, 0.0) - jnp.log1p(jnp.exp(-jnp.abs(f_pre)))
            k = jax.nn.sigmoid(-f_pre)
        else:
            log_f = jnp.log(lb + (1.0 - lb) * jax.nn.sigmoid(f_pre))
            k = (1.0 - lb) * jax.nn.sigmoid(-f_pre)
        if valid_len is not None:
            live = (j * tl + r0 + row) < valid_len
            log_f = jnp.where(live, log_f, 0.0)
            k = jnp.where(live, k, 0.0)
        b = log_f
        for sh in (1, 2, 4, 8):
            b = b + jnp.where(row >= sh, pltpu.roll(b, sh, 0), 0.0)

        for h in range(HG_HEADS):
            hs = slice(h * HG_HEAD_DIM, (h + 1) * HG_HEAD_DIM)
            qh, kh, vh, bh = q[:, hs], k[:, hs], v[:, hs], b[:, hs]
            b_last = bh[HG_CHUNK - 1:HG_CHUNK, :]
            st = st_scr[h]
            o = lax.dot_general((qh * jnp.exp(bh)).astype(BF16), st.astype(BF16),
                                (((1,), (1,)), ((), ())), preferred_element_type=F32)
            for s in range(HG_CHUNK):
                dec = jnp.exp(jnp.minimum(bh - bh[s:s + 1, :], 0.0))
                att = jnp.sum(qh * kh[s:s + 1, :] * dec, axis=-1, keepdims=True)
                att = jnp.where(row_col >= s, att, 0.0)
                o = o + att * vh[s:s + 1, :]
            k_dec = kh * jnp.exp(b_last - bh)
            upd = lax.dot_general(vh.astype(BF16), k_dec.astype(BF16),
                                  (((0,), (0,)), ((), ())), preferred_element_type=F32)
            st_scr[h] = st * jnp.exp(b_last) + upd
            o = o * lax.rsqrt(jnp.mean(o * o, axis=-1, keepdims=True) + RMS_EPS) * gw
            gh = g_out[:, hs]
            y_ref[0, pl.ds(r0, HG_CHUNK), hs] = o * (gh * jax.nn.sigmoid(gh))
        return carry

    lax.fori_loop(0, tl // HG_CHUNK, chunk_body, 0)

    @pl.when(j == n_t - 1)
    def _():
        for h in range(HG_HEADS):
            sout_ref[0, h] = st_scr[h].T


def _hgrn_branch(hg, s0_layers, layer, lb, gnorm_w, tl, valid_len, first_layer):
    n_b, n_l, _ = hg.shape
    st_shape = (HG_HEADS, HG_HEAD_DIM, HG_HEAD_DIM)
    return pl.pallas_call(
        functools.partial(_hgrn_kernel, tl=tl, valid_len=valid_len, first_layer=first_layer),
        grid=(n_b, n_l // tl),
        in_specs=[pl.BlockSpec((1, tl, 4 * HG_WIDTH), lambda b, j: (b, j, 0)),
                  pl.BlockSpec((1, 1) + st_shape, lambda b, j: (layer, b, 0, 0, 0)),
                  pl.BlockSpec((1, HG_WIDTH), lambda b, j: (0, 0)),
                  pl.BlockSpec((1, HG_HEAD_DIM), lambda b, j: (0, 0))],
        out_specs=[pl.BlockSpec((1, tl, HG_WIDTH), lambda b, j: (b, j, 0)),
                   pl.BlockSpec((1,) + st_shape, lambda b, j: (b, 0, 0, 0))],
        out_shape=[jax.ShapeDtypeStruct((n_b, n_l, HG_WIDTH), F32),
                   jax.ShapeDtypeStruct((n_b,) + st_shape, F32)],
        scratch_shapes=[pltpu.VMEM(st_shape, F32)],
        compiler_params=_cparams("parallel", "arbitrary"),
        name="hgrn_branch",
    )(hg, s0_layers, lb.reshape(1, HG_WIDTH), gnorm_w.reshape(1, HG_HEAD_DIM))


def _mix_kernel(ys5_ref, yhg_ref, gates_ref, x_ref, gate1_ref, shift2_ref, scale2_ref,
                wglu_ref, bglu_ref, wus_ref, wuh_ref, wout_ref, g_ref, b_ref, rw_ref, rb_ref,
                x1_ref, u2_ref, topi_ref, topw_ref):
    ys = ys5_ref[0]
    glu = ys * jax.nn.sigmoid(
        jnp.dot(ys.astype(BF16), wglu_ref[...], preferred_element_type=F32) + bglu_ref[...])
    gates = jax.nn.sigmoid(gates_ref[0])
    merged = (gates[:, :D_MODEL] * jnp.dot(glu.astype(BF16), wus_ref[...], preferred_element_type=F32)
              + gates[:, D_MODEL:] * jnp.dot(yhg_ref[0].astype(BF16), wuh_ref[...],
                                             preferred_element_type=F32))
    mix = jnp.dot(merged.astype(BF16), wout_ref[...], preferred_element_type=F32)
    x1 = _layer_norm(DEEPNORM_ALPHA * x_ref[0] + gate1_ref[0] * mix, g_ref[...], b_ref[...])
    x1_ref[0] = x1
    u2 = x1 * (1.0 + scale2_ref[0]) + shift2_ref[0]
    u2_ref[0] = u2
    logits = jnp.dot(u2, rw_ref[...], preferred_element_type=F32,
                     precision=lax.Precision.HIGHEST) + rb_ref[...]
    lane = lax.broadcasted_iota(jnp.int32, logits.shape, 1)
    lane_f = lane.astype(F32)
    vals, idxs = [], []
    for _ in range(TOP_K):
        m = jnp.max(logits, axis=-1, keepdims=True)
        idx = jnp.min(jnp.where(logits == m, lane_f, float(LANES)), axis=-1,
                      keepdims=True).astype(jnp.int32)
        vals.append(m)
        idxs.append(idx)
        logits = jnp.where(lane == idx, NEG_BIG, logits)
    exps = [jnp.exp(val - vals[0]) for val in vals]
    denom = exps[0] + exps[1] + exps[2] + exps[3]
    topi = jnp.full(lane.shape, -1, jnp.int32)
    topw = jnp.zeros(lane.shape, F32)
    for kk in range(TOP_K):
        topi = jnp.where(lane == kk, idxs[kk], topi)
        topw = jnp.where(lane == kk, exps[kk] / denom, topw)
    topi_ref[0] = topi
    topw_ref[0] = topw


def _mix(ys5, yhg, gates, x, gate1, shift2, scale2, lw, tm, per_row):
    n_g, n_r, _ = x.shape

    def row_spec(width):
        return pl.BlockSpec((1, tm, width), lambda g, i: (g, i, 0))

    def const(shape):
        return pl.BlockSpec(shape, lambda g, i: (0, 0))

    ms = _mod_spec(tm, per_row)
    return pl.pallas_call(
        _mix_kernel,
        grid=(n_g, n_r // tm),
        in_specs=[row_spec(S5_WIDTH), row_spec(HG_WIDTH), row_spec(2 * D_MODEL), row_spec(D_MODEL),
                  ms, ms, ms,
                  const((S5_WIDTH, S5_WIDTH)), const((1, S5_WIDTH)),
                  const((S5_WIDTH, D_MODEL)), const((HG_WIDTH, D_MODEL)),
                  const((D_MODEL, D_MODEL)), const((1, D_MODEL)), const((1, D_MODEL)),
                  const((D_MODEL, LANES)), const((1, LANES))],
        out_specs=[row_spec(D_MODEL), row_spec(D_MODEL), row_spec(LANES), row_spec(LANES)],
        out_shape=[jax.ShapeDtypeStruct((n_g, n_r, D_MODEL), F32),
                   jax.ShapeDtypeStruct((n_g, n_r, D_MODEL), F32),
                   jax.ShapeDtypeStruct((n_g, n_r, LANES), jnp.int32),
                   jax.ShapeDtypeStruct((n_g, n_r, LANES), F32)],
        compiler_params=_cparams("parallel", "parallel"),
        name="branch_mix",
    )(ys5, yhg, gates, x, gate1, shift2, scale2,
      lw['w_glu'], lw['b_glu'], lw['w_up_s5'], lw['w_up_hg'], lw['w_out'],
      lw['ln1_g'], lw['ln1_b'], lw['router_w'], lw['router_b'])


RANK_TILE = 384


def _expert_onehot(top_idx, kk):
    lane = lax.broadcasted_iota(jnp.int32, top_idx.shape, 1)
    return (top_idx[:, kk:kk + 1] == lane).astype(F32)


def _count_kernel(topi_ref, cnt_ref):
    @pl.when(pl.program_id(0) == 0)
    def _():
        cnt_ref[...] = jnp.zeros(cnt_ref.shape, F32)

    top_idx = topi_ref[...]
    acc = cnt_ref[...]
    for kk in range(TOP_K):
        acc = acc + jnp.sum(_expert_onehot(top_idx, kk), axis=0, keepdims=True)
    cnt_ref[...] = acc


def _rank_kernel(topi_ref, start_ref, dest_ref, off_scr):
    @pl.when(pl.program_id(0) == 0)
    def _():
        off_scr[...] = start_ref[...]

    top_idx = topi_ref[...]
    tm = top_idx.shape[0]
    lane = lax.broadcasted_iota(jnp.int32, top_idx.shape, 1)
    earlier = (lax.broadcasted_iota(jnp.int32, (tm, tm), 1)
               < lax.broadcasted_iota(jnp.int32, (tm, tm), 0)).astype(BF16)
    off = off_scr[...]
    dest = jnp.zeros(top_idx.shape, jnp.int32)
    for kk in range(TOP_K):
        onehot = _expert_onehot(top_idx, kk)
        before = jnp.dot(earlier, onehot.astype(BF16), preferred_element_type=F32)
        slot = jnp.sum(onehot * (before + off), axis=-1, keepdims=True)
        dest = jnp.where(lane == kk, slot.astype(jnp.int32), dest)
        off = off + jnp.sum(onehot, axis=0, keepdims=True)
    dest_ref[...] = dest
    off_scr[...] = off


def _slot_tables(top_idx_all):
    n_tok = top_idx_all.shape[0]
    n_tiles = n_tok // RANK_TILE
    tile_spec = pl.BlockSpec((RANK_TILE, LANES), lambda i: (i, 0))
    lane_spec = pl.BlockSpec((1, LANES), lambda i: (0, 0))
    counts = pl.pallas_call(
        _count_kernel, grid=(n_tiles,), in_specs=[tile_spec], out_specs=lane_spec,
        out_shape=jax.ShapeDtypeStruct((1, LANES), F32),
        compiler_params=_cparams("arbitrary"), name="expert_count",
    )(top_idx_all)
    counts = counts[0, :N_EXPERTS].astype(jnp.int32)
    padded = (counts + MOE_BLOCK - 1) // MOE_BLOCK * MOE_BLOCK
    pad_end = jnp.cumsum(padded)
    pad_start = pad_end - padded
    start_row = jnp.zeros((1, LANES), F32).at[0, :N_EXPERTS].set(pad_start.astype(F32))
    dest = pl.pallas_call(
        _rank_kernel, grid=(n_tiles,), in_specs=[tile_spec, lane_spec], out_specs=tile_spec,
        out_shape=jax.ShapeDtypeStruct((n_tok, LANES), jnp.int32),
        scratch_shapes=[pltpu.VMEM((1, LANES), F32)],
        compiler_params=_cparams("arbitrary"), name="expert_rank",
    )(top_idx_all, start_row)
    n_blocks = -(-(n_tok * TOP_K) // MOE_BLOCK) + N_EXPERTS
    block_lo = jnp.arange(n_blocks, dtype=jnp.int32) * MOE_BLOCK
    block_e = jnp.minimum(jnp.sum((pad_end[None, :] <= block_lo[:, None]).astype(jnp.int32), axis=1),
                          N_EXPERTS - 1).astype(jnp.int32)
    n_used = (pad_end[-1] // MOE_BLOCK).astype(jnp.int32).reshape(1)
    return dest[:, :TOP_K].reshape(-1), pad_end.astype(jnp.int32), block_e, n_used, n_blocks


def _dispatch_kernel(dest_ref, pad_end_ref, n_used_ref, up_ref, us_ref, xb_hbm, zero_scr, sem,
                     *, tm_p, n_tiles_p, n_blocks):
    i = pl.program_id(0)

    def zero_block(row0):
        return pltpu.make_async_copy(zero_scr, xb_hbm.at[pl.ds(row0, MOE_BLOCK)], sem)

    @pl.when(i == 0)
    def _():
        zero_scr[...] = jnp.zeros(zero_scr.shape, F32)

        def pad_block(e):
            start = pad_end_ref[e - 1] if e else 0
            end = pad_end_ref[e]
            return end > start, pl.multiple_of(jnp.maximum(end - MOE_BLOCK, 0), MOE_BLOCK)

        def tail_start(blk, carry):
            zero_block(pl.multiple_of(blk * MOE_BLOCK, MOE_BLOCK)).start()
            return carry

        def tail_wait(blk, carry):
            zero_block(pl.multiple_of(blk * MOE_BLOCK, MOE_BLOCK)).wait()
            return carry

        for e in range(N_EXPERTS):
            live, row0 = pad_block(e)
            pl.when(live)(lambda row0=row0: zero_block(row0).start())
        lax.fori_loop(n_used_ref[0], n_blocks, tail_start, 0)
        for e in range(N_EXPERTS):
            live, row0 = pad_block(e)
            pl.when(live)(lambda row0=row0: zero_block(row0).wait())
        lax.fori_loop(n_used_ref[0], n_blocks, tail_wait, 0)

    def scatter(src_ref, n_rows, first_tok):
        def issue(r, carry):
            for kk in range(TOP_K):
                slot = dest_ref[(first_tok + r) * TOP_K + kk]
                pltpu.make_async_copy(src_ref.at[pl.ds(r, 1)], xb_hbm.at[pl.ds(slot, 1)], sem).start()
            return carry

        lax.fori_loop(0, n_rows, issue, 0)
        for kk in range(TOP_K):
            pltpu.make_async_copy(src_ref, xb_hbm.at[pl.ds(0, n_rows)], sem).wait()

    @pl.when(i < n_tiles_p)
    def _():
        scatter(up_ref, tm_p, i * tm_p)

    @pl.when(i == n_tiles_p)
    def _():
        scatter(us_ref, us_ref.shape[0], n_tiles_p * tm_p)


def _dispatch(dest, pad_end, n_used, u_prompt, u_sample, n_blocks, tm_p):
    n_tiles_p = u_prompt.shape[0] // tm_p
    n_s = u_sample.shape[0]
    grid_spec = pltpu.PrefetchScalarGridSpec(
        num_scalar_prefetch=3, grid=(n_tiles_p + 1,),
        in_specs=[pl.BlockSpec((tm_p, D_MODEL), lambda i, d, p, nu: (jnp.minimum(i, n_tiles_p - 1), 0)),
                  pl.BlockSpec((n_s, D_MODEL), lambda i, d, p, nu: (0, 0))],
        out_specs=pl.BlockSpec(memory_space=pl.ANY),
        scratch_shapes=[pltpu.VMEM((MOE_BLOCK, D_MODEL), F32), pltpu.SemaphoreType.DMA(())])
    return pl.pallas_call(
        functools.partial(_dispatch_kernel, tm_p=tm_p, n_tiles_p=n_tiles_p, n_blocks=n_blocks),
        grid_spec=grid_spec,
        out_shape=jax.ShapeDtypeStruct((n_blocks * MOE_BLOCK, D_MODEL), F32),
        compiler_params=_cparams("arbitrary"),
        name="moe_dispatch",
    )(dest, pad_end, n_used, u_prompt, u_sample)


def _moe_kernel(block_e_ref, n_used_ref, x_ref, wgu_ref, bgu_ref, wd_ref, bd_ref, y_ref,
                wgu_bf, wd_bf):
    i = pl.program_id(0)
    n_used = n_used_ref[0]
    e = block_e_ref[i]
    e_prev = block_e_ref[jnp.maximum(i - 1, 0)]

    @pl.when(jnp.logical_and(i < n_used, jnp.logical_or(i == 0, e != e_prev)))
    def _():
        wgu_bf[...] = wgu_ref[0, 0].astype(BF16)
        wd_bf[...] = wd_ref[0, 0].astype(BF16)

    @pl.when(i < n_used)
    def _():
        h = jnp.dot(x_ref[...].astype(BF16), wgu_bf[...], preferred_element_type=F32) + bgu_ref[0, 0]
        gate = jnp.minimum(h[:, :D_EXPERT], SWIGLU_LIMIT)
        up = jnp.clip(h[:, D_EXPERT:], -SWIGLU_LIMIT, SWIGLU_LIMIT)
        act = gate * jax.nn.sigmoid(SWIGLU_ALPHA * gate) * (up + 1.0)
        y_ref[...] = jnp.dot(act.astype(BF16), wd_bf[...], preferred_element_type=F32) + bd_ref[0, 0]

    @pl.when(i >= n_used)
    def _():
        y_ref[...] = jnp.zeros(y_ref.shape, F32)


def _moe_experts(xb, block_e, n_used, layer, w_gate_up, b_gate_up, w_down, b_down):
    n_blocks = block_e.shape[0]

    def expert_spec(shape):
        return pl.BlockSpec((1, 1) + shape, lambda i, be, nu: (layer, be[i], 0, 0))

    grid_spec = pltpu.PrefetchScalarGridSpec(
        num_scalar_prefetch=2,
        grid=(n_blocks,),
        in_specs=[pl.BlockSpec((MOE_BLOCK, D_MODEL), lambda i, be, nu: (jnp.minimum(i, nu[0] - 1), 0)),
                  expert_spec((D_MODEL, 2 * D_EXPERT)), expert_spec((1, 2 * D_EXPERT)),
                  expert_spec((D_EXPERT, D_MODEL)), expert_spec((1, D_MODEL))],
        out_specs=pl.BlockSpec((MOE_BLOCK, D_MODEL), lambda i, be, nu: (i, 0)),
        scratch_shapes=[pltpu.VMEM((D_MODEL, 2 * D_EXPERT), BF16),
                        pltpu.VMEM((D_EXPERT, D_MODEL), BF16)],
    )
    return pl.pallas_call(
        _moe_kernel,
        grid_spec=grid_spec,
        out_shape=jax.ShapeDtypeStruct((n_blocks * MOE_BLOCK, D_MODEL), F32),
        compiler_params=_cparams("arbitrary"),
        name="moe_experts",
    )(block_e, n_used, xb, w_gate_up,
      b_gate_up.reshape(DEPTH, N_EXPERTS, 1, 2 * D_EXPERT),
      w_down, b_down.reshape(DEPTH, N_EXPERTS, 1, D_MODEL))


def _combine_kernel(dest_ref, yb_hbm, x1_ref, topw_ref, gate2_ref, g_ref, b_ref, o_ref, ybuf, sem,
                    *, tm, tok_offset):
    g = pl.program_id(0)
    i = pl.program_id(1)
    base = (tok_offset + (g * pl.num_programs(1) + i) * tm) * TOP_K

    def issue(r, carry):
        for kk in range(TOP_K):
            slot = dest_ref[base + r * TOP_K + kk]
            pltpu.make_async_copy(yb_hbm.at[pl.ds(slot, 1)], ybuf.at[kk, pl.ds(r, 1)], sem).start()
        return carry

    lax.fori_loop(0, tm, issue, 0)
    for kk in range(TOP_K):
        pltpu.make_async_copy(yb_hbm.at[pl.ds(0, tm)], ybuf.at[kk], sem).wait()
    topw = topw_ref[0]
    ffn = ((topw[:, 0:1] * ybuf[0] + topw[:, 1:2] * ybuf[1])
           + (topw[:, 2:3] * ybuf[2] + topw[:, 3:4] * ybuf[3]))
    o_ref[0] = _layer_norm(DEEPNORM_ALPHA * x1_ref[0] + gate2_ref[0] * ffn, g_ref[...], b_ref[...])


def _combine(dest, yb, x1, topw, gate2, ln_g, ln_b, tm, per_row, tok_offset):
    n_g, n_r, _ = x1.shape
    ms = (pl.BlockSpec((1, tm, D_MODEL), lambda g, i, d: (g, i, 0)) if per_row
          else pl.BlockSpec((1, 1, D_MODEL), lambda g, i, d: (g, 0, 0)))
    grid_spec = pltpu.PrefetchScalarGridSpec(
        num_scalar_prefetch=1,
        grid=(n_g, n_r // tm),
        in_specs=[pl.BlockSpec(memory_space=pl.ANY),
                  pl.BlockSpec((1, tm, D_MODEL), lambda g, i, d: (g, i, 0)),
                  pl.BlockSpec((1, tm, LANES), lambda g, i, d: (g, i, 0)),
                  ms,
                  pl.BlockSpec((1, D_MODEL), lambda g, i, d: (0, 0)),
                  pl.BlockSpec((1, D_MODEL), lambda g, i, d: (0, 0))],
        out_specs=pl.BlockSpec((1, tm, D_MODEL), lambda g, i, d: (g, i, 0)),
        scratch_shapes=[pltpu.VMEM((TOP_K, tm, D_MODEL), F32), pltpu.SemaphoreType.DMA(())],
    )
    return pl.pallas_call(
        functools.partial(_combine_kernel, tm=tm, tok_offset=tok_offset),
        grid_spec=grid_spec,
        out_shape=jax.ShapeDtypeStruct(x1.shape, F32),
        compiler_params=_cparams("arbitrary", "arbitrary"),
        name="moe_combine",
    )(dest, yb, x1, topw, gate2, ln_g, ln_b)


def kernel(x_prompt, x_sample, state_s5_re, state_s5_im, state_hgrn, c_prompt, c_sample,
           w_ada, b_ada, w_in, s5_lambda_re, s5_lambda_im, s5_log_dt, s5_b_re, s5_b_im,
           s5_c_re, s5_c_im, s5_d, w_glu, b_glu, w_up_s5, hgrn_lb_logits, hgrn_gnorm_w,
           w_up_hg, w_out, ln1_g, ln1_b, router_w, router_b, w_gate_up, b_gate_up,
           w_down, b_down, ln2_g, ln2_b):
    n_p, seq, _ = x_prompt.shape
    n_s = x_sample.shape[0]
    n_tok_p = n_p * seq

    lb_p = jax.nn.softmax(hgrn_lb_logits.astype(F32), axis=0)
    lb_all = jnp.cumsum(lb_p, axis=0) - lb_p[0]

    mod = _modulation(jnp.concatenate([c_prompt, c_sample], axis=0), w_ada, b_ada)

    def mods(layer, which):
        sl = slice(which * D_MODEL, (which + 1) * D_MODEL)
        return (mod[layer, :n_p, sl].reshape(n_p, 1, D_MODEL),
                mod[layer, n_p:, sl].reshape(1, n_s, D_MODEL))

    xp = x_prompt
    xs = x_sample.reshape(1, n_s, D_MODEL)
    zero_s5 = jnp.zeros((n_p, S5_FLAT), F32)
    zero_hg = jnp.zeros((1, n_p, HG_HEADS, HG_HEAD_DIM, HG_HEAD_DIM), F32)
    outs = {k: [] for k in ('p_re', 'p_im', 'p_hg', 's_re', 's_im', 's_hg')}

    for l in range(DEPTH):
        shift1, scale1, gate1, shift2, scale2, gate2 = [mods(l, w) for w in range(6)]
        w_in_bf = w_in[l].astype(BF16)
        bb, cc, a, d = _s5_params(s5_lambda_re[l], s5_lambda_im[l], s5_log_dt[l], s5_b_re[l],
                                  s5_b_im[l], s5_c_re[l], s5_c_im[l], s5_d[l])
        router_w_pad = jnp.zeros((D_MODEL, LANES), F32).at[:, :N_EXPERTS].set(router_w[l])
        router_b_pad = jnp.full((1, LANES), NEG_BIG, F32).at[0, :N_EXPERTS].set(router_b[l])
        lw = {'w_glu': w_glu[l].astype(BF16), 'b_glu': b_glu[l].reshape(1, S5_WIDTH),
              'w_up_s5': w_up_s5[l].astype(BF16), 'w_up_hg': w_up_hg[l].astype(BF16),
              'w_out': w_out[l].astype(BF16), 'ln1_g': ln1_g[l].reshape(1, D_MODEL),
              'ln1_b': ln1_b[l].reshape(1, D_MODEL), 'router_w': router_w_pad,
              'router_b': router_b_pad}
        first = l == 0

        us5_p, hg_p, gates_p = _input_proj(xp, shift1[0], scale1[0], w_in_bf, 256, False)
        ys5_p, pre, pim = _s5_branch(us5_p, zero_s5, zero_s5, bb, cc, a, d, 64)
        yhg_p, phg = _hgrn_branch(hg_p, zero_hg, 0, lb_all[l], hgrn_gnorm_w[l], 128, None, first)
        x1_p, u2_p, ti_p, tw_p = _mix(ys5_p, yhg_p, gates_p, xp, gate1[0], shift2[0], scale2[0],
                                      lw, 256, False)
        us5_s, hg_s, gates_s = _input_proj(xs, shift1[1], scale1[1], w_in_bf, n_s, True)
        ys5_s, sre, sim = _s5_branch(us5_s.reshape(n_s, 1, S5_WIDTH),
                                     state_s5_re[l].reshape(n_s, S5_FLAT),
                                     state_s5_im[l].reshape(n_s, S5_FLAT), bb, cc, a, d, 1)
        hg_s_pad = jnp.pad(hg_s.reshape(n_s, 1, 4 * HG_WIDTH), ((0, 0), (0, HG_CHUNK - 1), (0, 0)))
        yhg_s, shg = _hgrn_branch(hg_s_pad, state_hgrn, l, lb_all[l], hgrn_gnorm_w[l],
                                  HG_CHUNK, 1, first)
        x1_s, u2_s, ti_s, tw_s = _mix(ys5_s.reshape(1, n_s, S5_WIDTH),
                                      yhg_s[:, 0, :].reshape(1, n_s, HG_WIDTH), gates_s, xs,
                                      gate1[1], shift2[1], scale2[1], lw, n_s, True)

        top_idx_all = jnp.concatenate([ti_p.reshape(n_tok_p, LANES), ti_s.reshape(n_s, LANES)], axis=0)
        dest, pad_end, block_e, n_used, n_blocks = _slot_tables(top_idx_all)
        xb = _dispatch(dest, pad_end, n_used, u2_p.reshape(n_tok_p, D_MODEL),
                       u2_s.reshape(n_s, D_MODEL), n_blocks, 512)
        yb = _moe_experts(xb, block_e, n_used, l, w_gate_up, b_gate_up, w_down, b_down)
        g2, b2 = ln2_g[l].reshape(1, D_MODEL), ln2_b[l].reshape(1, D_MODEL)
        xp = _combine(dest, yb, x1_p, tw_p, gate2[0], g2, b2, 256, False, 0)
        xs = _combine(dest, yb, x1_s, tw_s, gate2[1], g2, b2, n_s, True, n_tok_p)

        outs['p_re'].append(pre.reshape(n_p, S5_GROUPS, S5_STATE))
        outs['p_im'].append(pim.reshape(n_p, S5_GROUPS, S5_STATE))
        outs['p_hg'].append(phg)
        outs['s_re'].append(sre.reshape(n_s, S5_GROUPS, S5_STATE))
        outs['s_im'].append(sim.reshape(n_s, S5_GROUPS, S5_STATE))
        outs['s_hg'].append(shg)

    return (xp, xs.reshape(n_s, 1, D_MODEL),
            jnp.stack(outs['p_re']), jnp.stack(outs['p_im']), jnp.stack(outs['p_hg']),
            jnp.stack(outs['s_re']), jnp.stack(outs['s_im']), jnp.stack(outs['s_hg']))
```

```python
import functools

import jax
import jax.numpy as jnp
from jax import lax
from jax.experimental import pallas as pl
from jax.experimental.pallas import tpu as pltpu

F32 = jnp.float32
BF16 = jnp.bfloat16

LANES = 128
SUBLANES = 8
MXU_TILE = 256
VMEM_LIMIT = 56 * 1024 * 1024
NEG_BIG = -1e30

D_MODEL = 1024
DEPTH = 2
S5_WIDTH = 512
S5_GROUP = 16
S5_GROUPS = 32
S5_STATE = 64
S5_FLAT = S5_GROUPS * S5_STATE
HG_WIDTH = 512
HG_HEAD_DIM = 128
HG_HEADS = 4
HG_CHUNK = SUBLANES
RMS_EPS = 1e-6
N_IN = S5_WIDTH + 4 * HG_WIDTH + 2 * D_MODEL
N_EXPERTS = 32
TOP_K = 4
D_EXPERT = D_MODEL
SWIGLU_LIMIT = 7.0
SWIGLU_ALPHA = 1.702
MOE_BLOCK = 256
DEEPNORM_ALPHA = (2 * DEPTH) ** 0.25
LN_EPS = 1e-5


def _cparams(*sem):
    return pltpu.CompilerParams(dimension_semantics=sem, vmem_limit_bytes=VMEM_LIMIT)


def _layer_norm(x, g, b):
    mu = jnp.mean(x, axis=-1, keepdims=True)
    xc = x - mu
    var = jnp.mean(xc * xc, axis=-1, keepdims=True)
    return xc * lax.rsqrt(var + LN_EPS) * g + b


def _mod_kernel(c_ref, w_ref, b_ref, o_ref):
    c = c_ref[...]
    s = c * jax.nn.sigmoid(c)
    o_ref[0] = jnp.dot(s.astype(BF16), w_ref[0].astype(BF16),
                       preferred_element_type=F32) + b_ref[0]


def _modulation(c_all, w_ada, b_ada):
    n_rows = c_all.shape[0]
    tn = 1536
    return pl.pallas_call(
        _mod_kernel,
        grid=(DEPTH, 6 * D_MODEL // tn),
        in_specs=[
            pl.BlockSpec((n_rows, D_MODEL), lambda l, j: (0, 0)),
            pl.BlockSpec((1, D_MODEL, tn), lambda l, j: (l, 0, j)),
            pl.BlockSpec((1, 1, tn), lambda l, j: (l, 0, j)),
        ],
        out_specs=pl.BlockSpec((1, n_rows, tn), lambda l, j: (l, 0, j)),
        out_shape=jax.ShapeDtypeStruct((DEPTH, n_rows, 6 * D_MODEL), F32),
        compiler_params=_cparams("parallel", "parallel"),
        name="adaln_mod",
    )(c_all, w_ada, b_ada.reshape(DEPTH, 1, 6 * D_MODEL))


def _proj_kernel(x_ref, shift_ref, scale_ref, w_ref, us5_ref, hg_ref, gates_ref):
    u = x_ref[0] * (1.0 + scale_ref[0]) + shift_ref[0]
    ub = u.astype(BF16)
    c0, c1 = S5_WIDTH, S5_WIDTH + 4 * HG_WIDTH
    us5_ref[0] = jnp.dot(ub, w_ref[:, :c0], preferred_element_type=F32)
    hg_ref[0] = jnp.dot(ub, w_ref[:, c0:c1], preferred_element_type=F32)
    gates_ref[0] = jnp.dot(ub, w_ref[:, c1:], preferred_element_type=F32)


def _mod_spec(tm, per_row):
    if per_row:
        return pl.BlockSpec((1, tm, D_MODEL), lambda g, i: (g, i, 0))
    return pl.BlockSpec((1, 1, D_MODEL), lambda g, i: (g, 0, 0))


def _input_proj(x, shift, scale, w_in_bf, tm, per_row):
    n_g, n_r, _ = x.shape

    def row_spec(width):
        return pl.BlockSpec((1, tm, width), lambda g, i: (g, i, 0))

    return pl.pallas_call(
        _proj_kernel,
        grid=(n_g, n_r // tm),
        in_specs=[row_spec(D_MODEL), _mod_spec(tm, per_row), _mod_spec(tm, per_row),
                  pl.BlockSpec((D_MODEL, N_IN), lambda g, i: (0, 0))],
        out_specs=[row_spec(S5_WIDTH), row_spec(4 * HG_WIDTH), row_spec(2 * D_MODEL)],
        out_shape=[jax.ShapeDtypeStruct((n_g, n_r, S5_WIDTH), F32),
                   jax.ShapeDtypeStruct((n_g, n_r, 4 * HG_WIDTH), F32),
                   jax.ShapeDtypeStruct((n_g, n_r, 2 * D_MODEL), F32)],
        compiler_params=_cparams("parallel", "parallel"),
        name="input_proj",
    )(x, shift, scale, w_in_bf)


S5_SLABS = S5_FLAT // LANES
S5_SCAN_SLABS = 8
S5_SCAN_UNROLL = 4


def _split_bf16x3(x):
    p1 = x.astype(BF16)
    r1 = x - p1.astype(F32)
    p2 = r1.astype(BF16)
    p3 = (r1 - p2.astype(F32)).astype(BF16)
    return jnp.concatenate([p1, p2, p3], axis=1)


def _s5_kernel(u_ref, h0re_ref, h0im_ref, bb_ref, cc_ref, a_ref, d_ref, perm_ref, perm_t_ref,
               y_ref, hre_ref, him_ref, bu_scr, hs_scr, h_scr, *, tl):
    j = pl.program_id(1)
    rows = SUBLANES * tl
    u = u_ref[...].reshape(rows, S5_WIDTH)
    ub = u.astype(BF16)
    if tl > 1:
        ub = jnp.dot(perm_ref[...], ub, preferred_element_type=F32).astype(BF16)

    @pl.when(j == 0)
    def _():
        h_scr[0] = h0re_ref[...]
        h_scr[1] = h0im_ref[...]

    cols_per_in_tile = MXU_TILE * (S5_STATE // S5_GROUP)
    for n in range(2 * S5_FLAT // MXU_TILE):
        kt = (n * MXU_TILE % S5_FLAT) // cols_per_in_tile
        res = jnp.dot(ub[:, kt * MXU_TILE:(kt + 1) * MXU_TILE],
                      bb_ref[kt * MXU_TILE:(kt + 1) * MXU_TILE, n * MXU_TILE:(n + 1) * MXU_TILE],
                      preferred_element_type=F32)
        bu_scr[2 * n] = res[:, :LANES]
        bu_scr[2 * n + 1] = res[:, LANES:]

    for c0 in range(0, S5_SLABS, S5_SCAN_SLABS):
        slabs = range(c0, c0 + S5_SCAN_SLABS)
        a_re = [jnp.broadcast_to(a_ref[0:1, c * LANES:(c + 1) * LANES], (SUBLANES, LANES)) for c in slabs]
        a_im = [jnp.broadcast_to(a_ref[1:2, c * LANES:(c + 1) * LANES], (SUBLANES, LANES)) for c in slabs]

        def step(t, carry, slabs=slabs, a_re=a_re, a_im=a_im):
            rsel = pl.ds(pl.multiple_of(t * SUBLANES, SUBLANES), SUBLANES)
            new = []
            for q, c in enumerate(slabs):
                h_re, h_im = carry[2 * q], carry[2 * q + 1]
                n_re = a_re[q] * h_re - a_im[q] * h_im + bu_scr[c, rsel, :]
                n_im = a_re[q] * h_im + a_im[q] * h_re + bu_scr[S5_SLABS + c, rsel, :]
                hs_scr[c, rsel, :] = n_re
                hs_scr[S5_SLABS + c, rsel, :] = n_im
                new += [n_re, n_im]
            return tuple(new)

        init = []
        for c in slabs:
            init += [h_scr[0, :, c * LANES:(c + 1) * LANES], h_scr[1, :, c * LANES:(c + 1) * LANES]]
        fin = lax.fori_loop(0, tl, step, tuple(init), unroll=min(S5_SCAN_UNROLL, tl))
        for q, c in enumerate(slabs):
            h_scr[0, :, c * LANES:(c + 1) * LANES] = fin[2 * q]
            h_scr[1, :, c * LANES:(c + 1) * LANES] = fin[2 * q + 1]

    slabs_per_out = S5_SLABS * MXU_TILE // S5_WIDTH
    y_tiles = []
    for m in range(S5_WIDTH // MXU_TILE):
        acc = None
        for part in range(2):
            for c in range(m * slabs_per_out, (m + 1) * slabs_per_out, 2):
                s = part * S5_SLABS + c
                hb = jnp.concatenate([hs_scr[s], hs_scr[s + 1]], axis=1).astype(BF16)
                term = jnp.dot(hb, cc_ref[s * LANES:(s + 2) * LANES, m * MXU_TILE:(m + 1) * MXU_TILE],
                               preferred_element_type=F32)
                acc = term if acc is None else acc + term
        y_tiles.append(acc)
    ch = jnp.concatenate(y_tiles, axis=1)
    if tl > 1:
        back = jnp.dot(perm_t_ref[...], _split_bf16x3(ch), preferred_element_type=F32)
        ch = back[:, :S5_WIDTH] + back[:, S5_WIDTH:2 * S5_WIDTH] + back[:, 2 * S5_WIDTH:]
    y = jax.nn.gelu(ch + d_ref[...] * u)
    y_ref[...] = y.reshape(y_ref.shape)
    hre_ref[...] = h_scr[0]
    him_ref[...] = h_scr[1]


def _s5_branch(u, h0_re, h0_im, bb, cc, a, d, tl):
    n_b, n_l, _ = u.shape
    rows = SUBLANES * tl
    if n_l == 1:
        u = u.reshape(n_b // SUBLANES, SUBLANES, S5_WIDTH)
        u_spec = pl.BlockSpec((None, SUBLANES, S5_WIDTH), lambda g, j: (g, 0, 0))
    else:
        u_spec = pl.BlockSpec((SUBLANES, tl, S5_WIDTH), lambda g, j: (g, j, 0))
    st_spec = pl.BlockSpec((SUBLANES, S5_FLAT), lambda g, j: (g, 0))

    def const(shape):
        return pl.BlockSpec(shape, lambda g, j: (0, 0))

    src = (jnp.arange(rows) % SUBLANES) * tl + jnp.arange(rows) // SUBLANES
    perm = (src[:, None] == jnp.arange(rows)[None, :]).astype(BF16)
    slab_scratch = pltpu.VMEM((2 * S5_SLABS, rows, LANES), F32)
    return pl.pallas_call(
        functools.partial(_s5_kernel, tl=tl),
        grid=(n_b // SUBLANES, n_l // tl),
        in_specs=[u_spec, st_spec, st_spec, const((S5_WIDTH, 2 * S5_FLAT)),
                  const((2 * S5_FLAT, S5_WIDTH)), const((2, S5_FLAT)), const((1, S5_WIDTH)),
                  const((rows, rows)), const((rows, rows))],
        out_specs=[u_spec, st_spec, st_spec],
        out_shape=[jax.ShapeDtypeStruct(u.shape, F32),
                   jax.ShapeDtypeStruct((n_b, S5_FLAT), F32),
                   jax.ShapeDtypeStruct((n_b, S5_FLAT), F32)],
        scratch_shapes=[slab_scratch, slab_scratch, pltpu.VMEM((2, SUBLANES, S5_FLAT), F32)],
        compiler_params=_cparams("parallel", "arbitrary"),
        name="s5_branch",
    )(u, h0_re, h0_im, bb, cc, a, d, perm, perm.T)


def _s5_params(lam_re, lam_im, log_dt, b_re, b_im, c_re, c_im, d_skip):
    dt = jnp.exp(log_dt)[:, None]
    mag = jnp.exp(lam_re * dt)
    ang = lam_im * dt
    ab_re, ab_im = mag * jnp.cos(ang), mag * jnp.sin(ang)
    den = jnp.square(lam_re) + jnp.square(lam_im)
    nr, ni = ab_re - 1.0, ab_im
    zf_re = (nr * lam_re + ni * lam_im) / den
    zf_im = (ni * lam_re - nr * lam_im) / den
    bb_re = zf_re[..., None] * b_re - zf_im[..., None] * b_im
    bb_im = zf_re[..., None] * b_im + zf_im[..., None] * b_re
    eye = jnp.eye(S5_GROUPS, dtype=F32)

    def in_blockdiag(m):
        return jnp.einsum('gph,gk->ghkp', m, eye).reshape(S5_WIDTH, S5_FLAT)

    def out_blockdiag(m):
        return jnp.einsum('ghp,gk->gpkh', m, eye).reshape(S5_FLAT, S5_WIDTH)

    bb = jnp.concatenate([in_blockdiag(bb_re), in_blockdiag(bb_im)], axis=1).astype(BF16)
    cc = jnp.concatenate([out_blockdiag(c_re), -out_blockdiag(c_im)], axis=0).astype(BF16)
    a = jnp.stack([ab_re.reshape(S5_FLAT), ab_im.reshape(S5_FLAT)])
    return bb, cc, a, d_skip.reshape(1, S5_WIDTH)


def _hgrn_kernel(hg_ref, s0_ref, lb_ref, gw_ref, y_ref, sout_ref, st_scr,
                 *, tl, valid_len, first_layer):
    j = pl.program_id(1)
    n_t = pl.num_programs(1)

    @pl.when(j == 0)
    def _():
        for h in range(HG_HEADS):
            st_scr[h] = s0_ref[0, 0, h].T

    row = lax.broadcasted_iota(jnp.int32, (HG_CHUNK, HG_WIDTH), 0)
    row_col = lax.broadcasted_iota(jnp.int32, (HG_CHUNK, 1), 0)
    lb = lb_ref[...]
    gw = gw_ref[...]

    def chunk_body(c, carry):
        r0 = pl.multiple_of(c * HG_CHUNK, HG_CHUNK)
        blk = hg_ref[0, pl.ds(r0, HG_CHUNK), :]
        q = blk[:, 0:HG_WIDTH]
        f_pre = blk[:, HG_WIDTH:2 * HG_WIDTH]
        v = blk[:, 2 * HG_WIDTH:3 * HG_WIDTH]
        g_out = blk[:, 3 * HG_WIDTH:4 * HG_WIDTH]
        if first_layer:
            log_f = jnp.minimum(f_pre, 0.0) - jnp.log1p(jnp.exp(-jnp.abs(f_pre)))
            k = jax.nn.sigmoid(-f_pre)
        else:
            log_f = jnp.log(lb + (1.0 - lb) * jax.nn.sigmoid(f_pre))
            k = (1.0 - lb) * jax.nn.sigmoid(-f_pre)
        if valid_len is not None:
            live = (j * tl + r0 + row) < valid_len
            log_f = jnp.where(live, log_f, 0.0)
            k = jnp.where(live, k, 0.0)
        b = log_f
        for sh in (1, 2, 4):
            b = b + jnp.where(row >= sh, pltpu.roll(b, sh, 0), 0.0)

        for h in range(HG_HEADS):
            hs = slice(h * HG_HEAD_DIM, (h + 1) * HG_HEAD_DIM)
            qh, kh, vh, bh = q[:, hs], k[:, hs], v[:, hs], b[:, hs]
            b_last = bh[HG_CHUNK - 1:HG_CHUNK, :]
            st = st_scr[h]
            o = lax.dot_general((qh * jnp.exp(bh)).astype(BF16), st.astype(BF16),
                                (((1,), (1,)), ((), ())), preferred_element_type=F32)
            for s in range(HG_CHUNK):
                dec = jnp.exp(jnp.minimum(bh - bh[s:s + 1, :], 0.0))
                att = jnp.sum(qh * kh[s:s + 1, :] * dec, axis=-1, keepdims=True)
                att = jnp.where(row_col >= s, att, 0.0)
                o = o + att * vh[s:s + 1, :]
            k_dec = kh * jnp.exp(b_last - bh)
            upd = lax.dot_general(vh.astype(BF16), k_dec.astype(BF16),
                                  (((0,), (0,)), ((), ())), preferred_element_type=F32)
            st_scr[h] = st * jnp.exp(b_last) + upd
            o = o * lax.rsqrt(jnp.mean(o * o, axis=-1, keepdims=True) + RMS_EPS) * gw
            gh = g_out[:, hs]
            y_ref[0, pl.ds(r0, HG_CHUNK), hs] = o * (gh * jax.nn.sigmoid(gh))
        return carry

    lax.fori_loop(0, tl // HG_CHUNK, chunk_body, 0)

    @pl.when(j == n_t - 1)
    def _():
        for h in range(HG_HEADS):
            sout_ref[0, h] = st_scr[h].T


def _hgrn_branch(hg, s0_layers, layer, lb, gnorm_w, tl, valid_len, first_layer):
    n_b, n_l, _ = hg.shape
    st_shape = (HG_HEADS, HG_HEAD_DIM, HG_HEAD_DIM)
    return pl.pallas_call(
        functools.partial(_hgrn_kernel, tl=tl, valid_len=valid_len, first_layer=first_layer),
        grid=(n_b, n_l // tl),
        in_specs=[pl.BlockSpec((1, tl, 4 * HG_WIDTH), lambda b, j: (b, j, 0)),
                  pl.BlockSpec((1, 1) + st_shape, lambda b, j: (layer, b, 0, 0, 0)),
                  pl.BlockSpec((1, HG_WIDTH), lambda b, j: (0, 0)),
                  pl.BlockSpec((1, HG_HEAD_DIM), lambda b, j: (0, 0))],
        out_specs=[pl.BlockSpec((1, tl, HG_WIDTH), lambda b, j: (b, j, 0)),
                   pl.BlockSpec((1,) + st_shape, lambda b, j: (b, 0, 0, 0))],
        out_shape=[jax.ShapeDtypeStruct((n_b, n_l, HG_WIDTH), F32),
                   jax.ShapeDtypeStruct((n_b,) + st_shape, F32)],
        scratch_shapes=[pltpu.VMEM(st_shape, F32)],
        compiler_params=_cparams("parallel", "arbitrary"),
        name="hgrn_branch",
    )(hg, s0_layers, lb.reshape(1, HG_WIDTH), gnorm_w.reshape(1, HG_HEAD_DIM))


def _mix_kernel(ys5_ref, yhg_ref, gates_ref, x_ref, gate1_ref, shift2_ref, scale2_ref,
                wglu_ref, bglu_ref, wus_ref, wuh_ref, wout_ref, g_ref, b_ref, rw_ref, rb_ref,
                x1_ref, u2_ref, topi_ref, topw_ref):
    ys = ys5_ref[0]
    glu = ys * jax.nn.sigmoid(
        jnp.dot(ys.astype(BF16), wglu_ref[...], preferred_element_type=F32) + bglu_ref[...])
    gates = jax.nn.sigmoid(gates_ref[0])
    merged = (gates[:, :D_MODEL] * jnp.dot(glu.astype(BF16), wus_ref[...], preferred_element_type=F32)
              + gates[:, D_MODEL:] * jnp.dot(yhg_ref[0].astype(BF16), wuh_ref[...],
                                             preferred_element_type=F32))
    mix = jnp.dot(merged.astype(BF16), wout_ref[...], preferred_element_type=F32)
    x1 = _layer_norm(DEEPNORM_ALPHA * x_ref[0] + gate1_ref[0] * mix, g_ref[...], b_ref[...])
    x1_ref[0] = x1
    u2 = x1 * (1.0 + scale2_ref[0]) + shift2_ref[0]
    u2_ref[0] = u2
    u_hi = u2.astype(BF16)
    u_lo = (u2 - u_hi.astype(F32)).astype(BF16)
    hh_hl = jnp.dot(u_hi, rw_ref[...], preferred_element_type=F32)
    lh = jnp.dot(u_lo, rw_ref[:, :LANES], preferred_element_type=F32)
    logits = (hh_hl[:, :LANES] + hh_hl[:, LANES:]) + lh + rb_ref[...]
    lane = lax.broadcasted_iota(jnp.int32, logits.shape, 1)
    lane_f = lane.astype(F32)
    vals, idxs = [], []
    for _ in range(TOP_K):
        m = jnp.max(logits, axis=-1, keepdims=True)
        idx = jnp.min(jnp.where(logits == m, lane_f, float(LANES)), axis=-1,
                      keepdims=True).astype(jnp.int32)
        vals.append(m)
        idxs.append(idx)
        logits = jnp.where(lane == idx, NEG_BIG, logits)
    exps = [jnp.exp(val - vals[0]) for val in vals]
    denom = exps[0] + exps[1] + exps[2] + exps[3]
    topi = jnp.full(lane.shape, -1, jnp.int32)
    topw = jnp.zeros(lane.shape, F32)
    for kk in range(TOP_K):
        topi = jnp.where(lane == kk, idxs[kk], topi)
        topw = jnp.where(lane == kk, exps[kk] / denom, topw)
    topi_ref[0] = topi
    topw_ref[0] = topw


def _mix(ys5, yhg, gates, x, gate1, shift2, scale2, lw, tm, per_row):
    n_g, n_r, _ = x.shape

    def row_spec(width):
        return pl.BlockSpec((1, tm, width), lambda g, i: (g, i, 0))

    def const(shape):
        return pl.BlockSpec(shape, lambda g, i: (0, 0))

    ms = _mod_spec(tm, per_row)
    return pl.pallas_call(
        _mix_kernel,
        grid=(n_g, n_r // tm),
        in_specs=[row_spec(S5_WIDTH), row_spec(HG_WIDTH), row_spec(2 * D_MODEL), row_spec(D_MODEL),
                  ms, ms, ms,
                  const((S5_WIDTH, S5_WIDTH)), const((1, S5_WIDTH)),
                  const((S5_WIDTH, D_MODEL)), const((HG_WIDTH, D_MODEL)),
                  const((D_MODEL, D_MODEL)), const((1, D_MODEL)), const((1, D_MODEL)),
                  const((D_MODEL, 2 * LANES)), const((1, LANES))],
        out_specs=[row_spec(D_MODEL), row_spec(D_MODEL), row_spec(LANES), row_spec(LANES)],
        out_shape=[jax.ShapeDtypeStruct((n_g, n_r, D_MODEL), F32),
                   jax.ShapeDtypeStruct((n_g, n_r, D_MODEL), F32),
                   jax.ShapeDtypeStruct((n_g, n_r, LANES), jnp.int32),
                   jax.ShapeDtypeStruct((n_g, n_r, LANES), F32)],
        compiler_params=_cparams("parallel", "parallel"),
        name="branch_mix",
    )(ys5, yhg, gates, x, gate1, shift2, scale2,
      lw['w_glu'], lw['b_glu'], lw['w_up_s5'], lw['w_up_hg'], lw['w_out'],
      lw['ln1_g'], lw['ln1_b'], lw['router_w'], lw['router_b'])


RANK_TILE = 384


def _expert_onehot(top_idx, kk):
    lane = lax.broadcasted_iota(jnp.int32, top_idx.shape, 1)
    return (top_idx[:, kk:kk + 1] == lane).astype(F32)


def _count_kernel(topi_ref, cnt_ref):
    @pl.when(pl.program_id(0) == 0)
    def _():
        cnt_ref[...] = jnp.zeros(cnt_ref.shape, F32)

    top_idx = topi_ref[...]
    acc = cnt_ref[...]
    for kk in range(TOP_K):
        acc = acc + jnp.sum(_expert_onehot(top_idx, kk), axis=0, keepdims=True)
    cnt_ref[...] = acc


def _rank_kernel(topi_ref, start_ref, dest_ref, off_scr):
    @pl.when(pl.program_id(0) == 0)
    def _():
        off_scr[...] = start_ref[...]

    top_idx = topi_ref[...]
    tm = top_idx.shape[0]
    lane = lax.broadcasted_iota(jnp.int32, top_idx.shape, 1)
    earlier = (lax.broadcasted_iota(jnp.int32, (tm, tm), 1)
               < lax.broadcasted_iota(jnp.int32, (tm, tm), 0)).astype(BF16)
    off = off_scr[...]
    dest = jnp.zeros(top_idx.shape, jnp.int32)
    for kk in range(TOP_K):
        onehot = _expert_onehot(top_idx, kk)
        before = jnp.dot(earlier, onehot.astype(BF16), preferred_element_type=F32)
        slot = jnp.sum(onehot * (before + off), axis=-1, keepdims=True)
        dest = jnp.where(lane == kk, slot.astype(jnp.int32), dest)
        off = off + jnp.sum(onehot, axis=0, keepdims=True)
    dest_ref[...] = dest
    off_scr[...] = off


def _slot_tables(top_idx_all):
    n_tok = top_idx_all.shape[0]
    n_tiles = n_tok // RANK_TILE
    tile_spec = pl.BlockSpec((RANK_TILE, LANES), lambda i: (i, 0))
    lane_spec = pl.BlockSpec((1, LANES), lambda i: (0, 0))
    counts = pl.pallas_call(
        _count_kernel, grid=(n_tiles,), in_specs=[tile_spec], out_specs=lane_spec,
        out_shape=jax.ShapeDtypeStruct((1, LANES), F32),
        compiler_params=_cparams("arbitrary"), name="expert_count",
    )(top_idx_all)
    counts = counts[0, :N_EXPERTS].astype(jnp.int32)
    padded = (counts + MOE_BLOCK - 1) // MOE_BLOCK * MOE_BLOCK
    pad_end = jnp.cumsum(padded)
    pad_start = pad_end - padded
    start_row = jnp.zeros((1, LANES), F32).at[0, :N_EXPERTS].set(pad_start.astype(F32))
    dest = pl.pallas_call(
        _rank_kernel, grid=(n_tiles,), in_specs=[tile_spec, lane_spec], out_specs=tile_spec,
        out_shape=jax.ShapeDtypeStruct((n_tok, LANES), jnp.int32),
        scratch_shapes=[pltpu.VMEM((1, LANES), F32)],
        compiler_params=_cparams("arbitrary"), name="expert_rank",
    )(top_idx_all, start_row)
    n_blocks = -(-(n_tok * TOP_K) // MOE_BLOCK) + N_EXPERTS
    block_lo = jnp.arange(n_blocks, dtype=jnp.int32) * MOE_BLOCK
    block_e = jnp.minimum(jnp.sum((pad_end[None, :] <= block_lo[:, None]).astype(jnp.int32), axis=1),
                          N_EXPERTS - 1).astype(jnp.int32)
    n_used = (pad_end[-1] // MOE_BLOCK).astype(jnp.int32).reshape(1)
    return dest[:, :TOP_K].reshape(-1), pad_end.astype(jnp.int32), block_e, n_used, n_blocks


def _dispatch_kernel(dest_ref, pad_end_ref, n_used_ref, up_ref, us_ref, xb_hbm, zero_scr, sem,
                     *, tm_p, n_tiles_p, n_blocks):
    i = pl.program_id(0)

    def zero_block(row0):
        return pltpu.make_async_copy(zero_scr, xb_hbm.at[pl.ds(row0, MOE_BLOCK)], sem)

    @pl.when(i == 0)
    def _():
        zero_scr[...] = jnp.zeros(zero_scr.shape, F32)

        def pad_block(e):
            start = pad_end_ref[e - 1] if e else 0
            end = pad_end_ref[e]
            return end > start, pl.multiple_of(jnp.maximum(end - MOE_BLOCK, 0), MOE_BLOCK)

        def tail_start(blk, carry):
            zero_block(pl.multiple_of(blk * MOE_BLOCK, MOE_BLOCK)).start()
            return carry

        def tail_wait(blk, carry):
            zero_block(pl.multiple_of(blk * MOE_BLOCK, MOE_BLOCK)).wait()
            return carry

        for e in range(N_EXPERTS):
            live, row0 = pad_block(e)
            pl.when(live)(lambda row0=row0: zero_block(row0).start())
        lax.fori_loop(n_used_ref[0], n_blocks, tail_start, 0)
        for e in range(N_EXPERTS):
            live, row0 = pad_block(e)
            pl.when(live)(lambda row0=row0: zero_block(row0).wait())
        lax.fori_loop(n_used_ref[0], n_blocks, tail_wait, 0)

    def scatter(src_ref, n_rows, first_tok):
        def issue(r, carry):
            for kk in range(TOP_K):
                slot = dest_ref[(first_tok + r) * TOP_K + kk]
                pltpu.make_async_copy(src_ref.at[pl.ds(r, 1)], xb_hbm.at[pl.ds(slot, 1)], sem).start()
            return carry

        lax.fori_loop(0, n_rows, issue, 0)
        for kk in range(TOP_K):
            pltpu.make_async_copy(src_ref, xb_hbm.at[pl.ds(0, n_rows)], sem).wait()

    @pl.when(i < n_tiles_p)
    def _():
        scatter(up_ref, tm_p, i * tm_p)

    @pl.when(i == n_tiles_p)
    def _():
        scatter(us_ref, us_ref.shape[0], n_tiles_p * tm_p)


def _dispatch(dest, pad_end, n_used, u_prompt, u_sample, n_blocks, tm_p):
    n_tiles_p = u_prompt.shape[0] // tm_p
    n_s = u_sample.shape[0]
    grid_spec = pltpu.PrefetchScalarGridSpec(
        num_scalar_prefetch=3, grid=(n_tiles_p + 1,),
        in_specs=[pl.BlockSpec((tm_p, D_MODEL), lambda i, d, p, nu: (jnp.minimum(i, n_tiles_p - 1), 0)),
                  pl.BlockSpec((n_s, D_MODEL), lambda i, d, p, nu: (0, 0))],
        out_specs=pl.BlockSpec(memory_space=pl.ANY),
        scratch_shapes=[pltpu.VMEM((MOE_BLOCK, D_MODEL), F32), pltpu.SemaphoreType.DMA(())])
    return pl.pallas_call(
        functools.partial(_dispatch_kernel, tm_p=tm_p, n_tiles_p=n_tiles_p, n_blocks=n_blocks),
        grid_spec=grid_spec,
        out_shape=jax.ShapeDtypeStruct((n_blocks * MOE_BLOCK, D_MODEL), F32),
        compiler_params=_cparams("arbitrary"),
        name="moe_dispatch",
    )(dest, pad_end, n_used, u_prompt, u_sample)


def _moe_kernel(block_e_ref, n_used_ref, x_ref, wgu_ref, bgu_ref, wd_ref, bd_ref, y_ref,
                wgu_bf, wd_bf):
    i = pl.program_id(0)
    n_used = n_used_ref[0]
    e = block_e_ref[i]
    e_prev = block_e_ref[jnp.maximum(i - 1, 0)]

    @pl.when(jnp.logical_and(i < n_used, jnp.logical_or(i == 0, e != e_prev)))
    def _():
        wgu_bf[...] = wgu_ref[0, 0].astype(BF16)
        wd_bf[...] = wd_ref[0, 0].astype(BF16)

    @pl.when(i < n_used)
    def _():
        h = jnp.dot(x_ref[...].astype(BF16), wgu_bf[...], preferred_element_type=F32) + bgu_ref[0, 0]
        gate = jnp.minimum(h[:, :D_EXPERT], SWIGLU_LIMIT)
        up = jnp.clip(h[:, D_EXPERT:], -SWIGLU_LIMIT, SWIGLU_LIMIT)
        act = gate * jax.nn.sigmoid(SWIGLU_ALPHA * gate) * (up + 1.0)
        y_ref[...] = jnp.dot(act.astype(BF16), wd_bf[...], preferred_element_type=F32) + bd_ref[0, 0]

    @pl.when(i >= n_used)
    def _():
        y_ref[...] = jnp.zeros(y_ref.shape, F32)


def _moe_experts(xb, block_e, n_used, layer, w_gate_up, b_gate_up, w_down, b_down):
    n_blocks = block_e.shape[0]

    def expert_spec(shape):
        return pl.BlockSpec((1, 1) + shape, lambda i, be, nu: (layer, be[i], 0, 0))

    grid_spec = pltpu.PrefetchScalarGridSpec(
        num_scalar_prefetch=2,
        grid=(n_blocks,),
        in_specs=[pl.BlockSpec((MOE_BLOCK, D_MODEL), lambda i, be, nu: (jnp.minimum(i, nu[0] - 1), 0)),
                  expert_spec((D_MODEL, 2 * D_EXPERT)), expert_spec((1, 2 * D_EXPERT)),
                  expert_spec((D_EXPERT, D_MODEL)), expert_spec((1, D_MODEL))],
        out_specs=pl.BlockSpec((MOE_BLOCK, D_MODEL), lambda i, be, nu: (i, 0)),
        scratch_shapes=[pltpu.VMEM((D_MODEL, 2 * D_EXPERT), BF16),
                        pltpu.VMEM((D_EXPERT, D_MODEL), BF16)],
    )
    return pl.pallas_call(
        _moe_kernel,
        grid_spec=grid_spec,
        out_shape=jax.ShapeDtypeStruct((n_blocks * MOE_BLOCK, D_MODEL), F32),
        compiler_params=_cparams("arbitrary"),
        name="moe_experts",
    )(block_e, n_used, xb, w_gate_up,
      b_gate_up.reshape(DEPTH, N_EXPERTS, 1, 2 * D_EXPERT),
      w_down, b_down.reshape(DEPTH, N_EXPERTS, 1, D_MODEL))


def _combine_kernel(dest_ref, yb_hbm, x1_ref, topw_ref, gate2_ref, g_ref, b_ref, o_ref, ybuf, sem,
                    *, tm, tok_offset):
    g = pl.program_id(0)
    i = pl.program_id(1)
    base = (tok_offset + (g * pl.num_programs(1) + i) * tm) * TOP_K

    def issue(r, carry):
        for kk in range(TOP_K):
            slot = dest_ref[base + r * TOP_K + kk]
            pltpu.make_async_copy(yb_hbm.at[pl.ds(slot, 1)], ybuf.at[kk, pl.ds(r, 1)], sem).start()
        return carry

    lax.fori_loop(0, tm, issue, 0)
    for kk in range(TOP_K):
        pltpu.make_async_copy(yb_hbm.at[pl.ds(0, tm)], ybuf.at[kk], sem).wait()
    topw = topw_ref[0]
    ffn = ((topw[:, 0:1] * ybuf[0] + topw[:, 1:2] * ybuf[1])
           + (topw[:, 2:3] * ybuf[2] + topw[:, 3:4] * ybuf[3]))
    o_ref[0] = _layer_norm(DEEPNORM_ALPHA * x1_ref[0] + gate2_ref[0] * ffn, g_ref[...], b_ref[...])


def _combine(dest, yb, x1, topw, gate2, ln_g, ln_b, tm, per_row, tok_offset):
    n_g, n_r, _ = x1.shape
    ms = (pl.BlockSpec((1, tm, D_MODEL), lambda g, i, d: (g, i, 0)) if per_row
          else pl.BlockSpec((1, 1, D_MODEL), lambda g, i, d: (g, 0, 0)))
    grid_spec = pltpu.PrefetchScalarGridSpec(
        num_scalar_prefetch=1,
        grid=(n_g, n_r // tm),
        in_specs=[pl.BlockSpec(memory_space=pl.ANY),
                  pl.BlockSpec((1, tm, D_MODEL), lambda g, i, d: (g, i, 0)),
                  pl.BlockSpec((1, tm, LANES), lambda g, i, d: (g, i, 0)),
                  ms,
                  pl.BlockSpec((1, D_MODEL), lambda g, i, d: (0, 0)),
                  pl.BlockSpec((1, D_MODEL), lambda g, i, d: (0, 0))],
        out_specs=pl.BlockSpec((1, tm, D_MODEL), lambda g, i, d: (g, i, 0)),
        scratch_shapes=[pltpu.VMEM((TOP_K, tm, D_MODEL), F32), pltpu.SemaphoreType.DMA(())],
    )
    return pl.pallas_call(
        functools.partial(_combine_kernel, tm=tm, tok_offset=tok_offset),
        grid_spec=grid_spec,
        out_shape=jax.ShapeDtypeStruct(x1.shape, F32),
        compiler_params=_cparams("arbitrary", "arbitrary"),
        name="moe_combine",
    )(dest, yb, x1, topw, gate2, ln_g, ln_b)


def kernel(x_prompt, x_sample, state_s5_re, state_s5_im, state_hgrn, c_prompt, c_sample,
           w_ada, b_ada, w_in, s5_lambda_re, s5_lambda_im, s5_log_dt, s5_b_re, s5_b_im,
           s5_c_re, s5_c_im, s5_d, w_glu, b_glu, w_up_s5, hgrn_lb_logits, hgrn_gnorm_w,
           w_up_hg, w_out, ln1_g, ln1_b, router_w, router_b, w_gate_up, b_gate_up,
           w_down, b_down, ln2_g, ln2_b):
    n_p, seq, _ = x_prompt.shape
    n_s = x_sample.shape[0]
    n_tok_p = n_p * seq

    lb_p = jax.nn.softmax(hgrn_lb_logits.astype(F32), axis=0)
    lb_all = jnp.cumsum(lb_p, axis=0) - lb_p[0]

    mod = _modulation(jnp.concatenate([c_prompt, c_sample], axis=0), w_ada, b_ada)

    def mods(layer, which):
        sl = slice(which * D_MODEL, (which + 1) * D_MODEL)
        return (mod[layer, :n_p, sl].reshape(n_p, 1, D_MODEL),
                mod[layer, n_p:, sl].reshape(1, n_s, D_MODEL))

    xp = x_prompt
    xs = x_sample.reshape(1, n_s, D_MODEL)
    zero_s5 = jnp.zeros((n_p, S5_FLAT), F32)
    zero_hg = jnp.zeros((1, n_p, HG_HEADS, HG_HEAD_DIM, HG_HEAD_DIM), F32)
    outs = {k: [] for k in ('p_re', 'p_im', 'p_hg', 's_re', 's_im', 's_hg')}

    for l in range(DEPTH):
        shift1, scale1, gate1, shift2, scale2, gate2 = [mods(l, w) for w in range(6)]
        w_in_bf = w_in[l].astype(BF16)
        bb, cc, a, d = _s5_params(s5_lambda_re[l], s5_lambda_im[l], s5_log_dt[l], s5_b_re[l],
                                  s5_b_im[l], s5_c_re[l], s5_c_im[l], s5_d[l])
        router_w_pad = jnp.zeros((D_MODEL, LANES), F32).at[:, :N_EXPERTS].set(router_w[l])
        router_w_hi = router_w_pad.astype(BF16)
        router_w_lo = (router_w_pad - router_w_hi.astype(F32)).astype(BF16)
        router_b_pad = jnp.full((1, LANES), NEG_BIG, F32).at[0, :N_EXPERTS].set(router_b[l])
        lw = {'w_glu': w_glu[l].astype(BF16), 'b_glu': b_glu[l].reshape(1, S5_WIDTH),
              'w_up_s5': w_up_s5[l].astype(BF16), 'w_up_hg': w_up_hg[l].astype(BF16),
              'w_out': w_out[l].astype(BF16), 'ln1_g': ln1_g[l].reshape(1, D_MODEL),
              'ln1_b': ln1_b[l].reshape(1, D_MODEL), 'router_w': jnp.concatenate([router_w_hi, router_w_lo], axis=1),
              'router_b': router_b_pad}
        first = l == 0

        us5_p, hg_p, gates_p = _input_proj(xp, shift1[0], scale1[0], w_in_bf, 256, False)
        ys5_p, pre, pim = _s5_branch(us5_p, zero_s5, zero_s5, bb, cc, a, d, 64)
        yhg_p, phg = _hgrn_branch(hg_p, zero_hg, 0, lb_all[l], hgrn_gnorm_w[l], 128, None, first)
        x1_p, u2_p, ti_p, tw_p = _mix(ys5_p, yhg_p, gates_p, xp, gate1[0], shift2[0], scale2[0],
                                      lw, 256, False)
        us5_s, hg_s, gates_s = _input_proj(xs, shift1[1], scale1[1], w_in_bf, n_s, True)
        ys5_s, sre, sim = _s5_branch(us5_s.reshape(n_s, 1, S5_WIDTH),
                                     state_s5_re[l].reshape(n_s, S5_FLAT),
                                     state_s5_im[l].reshape(n_s, S5_FLAT), bb, cc, a, d, 1)
        hg_s_pad = jnp.pad(hg_s.reshape(n_s, 1, 4 * HG_WIDTH), ((0, 0), (0, HG_CHUNK - 1), (0, 0)))
        yhg_s, shg = _hgrn_branch(hg_s_pad, state_hgrn, l, lb_all[l], hgrn_gnorm_w[l],
                                  HG_CHUNK, 1, first)
        x1_s, u2_s, ti_s, tw_s = _mix(ys5_s.reshape(1, n_s, S5_WIDTH),
                                      yhg_s[:, 0, :].reshape(1, n_s, HG_WIDTH), gates_s, xs,
                                      gate1[1], shift2[1], scale2[1], lw, n_s, True)

        top_idx_all = jnp.concatenate([ti_p.reshape(n_tok_p, LANES), ti_s.reshape(n_s, LANES)], axis=0)
        dest, pad_end, block_e, n_used, n_blocks = _slot_tables(top_idx_all)
        xb = _dispatch(dest, pad_end, n_used, u2_p.reshape(n_tok_p, D_MODEL),
                       u2_s.reshape(n_s, D_MODEL), n_blocks, 512)
        yb = _moe_experts(xb, block_e, n_used, l, w_gate_up, b_gate_up, w_down, b_down)
        g2, b2 = ln2_g[l].reshape(1, D_MODEL), ln2_b[l].reshape(1, D_MODEL)
        xp = _combine(dest, yb, x1_p, tw_p, gate2[0], g2, b2, 256, False, 0)
        xs = _combine(dest, yb, x1_s, tw_s, gate2[1], g2, b2, n_s, True, n_tok_p)

        outs['p_re'].append(pre.reshape(n_p, S5_GROUPS, S5_STATE))
        outs['p_im'].append(pim.reshape(n_p, S5_GROUPS, S5_STATE))
        outs['p_hg'].append(phg)
        outs['s_re'].append(sre.reshape(n_s, S5_GROUPS, S5_STATE))
        outs['s_im'].append(sim.reshape(n_s, S5_GROUPS, S5_STATE))
        outs['s_hg'].append(shg)

    return (xp, xs.reshape(n_s, 1, D_MODEL),
            jnp.stack(outs['p_re']), jnp.stack(outs['p_im']), jnp.stack(outs['p_hg']),
            jnp.stack(outs['s_re']), jnp.stack(outs['s_im']), jnp.stack(outs['s_hg']))
```

```python
import functools

import jax
import jax.numpy as jnp
from jax import lax
from jax.experimental import pallas as pl
from jax.experimental.pallas import tpu as pltpu

F32 = jnp.float32
BF16 = jnp.bfloat16

LANES = 128
SUBLANES = 8
MXU_TILE = 256
VMEM_LIMIT = 56 * 1024 * 1024
NEG_BIG = -1e30

D_MODEL = 1024
DEPTH = 2
S5_WIDTH = 512
S5_GROUP = 16
S5_GROUPS = 32
S5_STATE = 64
S5_FLAT = S5_GROUPS * S5_STATE
HG_WIDTH = 512
HG_HEAD_DIM = 128
HG_HEADS = 4
HG_CHUNK = 16
HG_UNROLL = 2
RMS_EPS = 1e-6
N_IN = S5_WIDTH + 4 * HG_WIDTH + 2 * D_MODEL
N_EXPERTS = 32
TOP_K = 4
D_EXPERT = D_MODEL
SWIGLU_LIMIT = 7.0
SWIGLU_ALPHA = 1.702
MOE_BLOCK = 256
DMA_ISSUE_UNROLL = 4
DEEPNORM_ALPHA = (2 * DEPTH) ** 0.25
LN_EPS = 1e-5


def _cparams(*sem):
    return pltpu.CompilerParams(dimension_semantics=sem, vmem_limit_bytes=VMEM_LIMIT)


def _layer_norm(x, g, b):
    mu = jnp.mean(x, axis=-1, keepdims=True)
    xc = x - mu
    var = jnp.mean(xc * xc, axis=-1, keepdims=True)
    return xc * lax.rsqrt(var + LN_EPS) * g + b


def _mod_kernel(c_ref, w_ref, b_ref, o_ref):
    c = c_ref[...]
    s = c * jax.nn.sigmoid(c)
    o_ref[0] = jnp.dot(s.astype(BF16), w_ref[0].astype(BF16),
                       preferred_element_type=F32) + b_ref[0]


def _modulation(c_all, w_ada, b_ada):
    n_rows = c_all.shape[0]
    tn = 1536
    return pl.pallas_call(
        _mod_kernel,
        grid=(DEPTH, 6 * D_MODEL // tn),
        in_specs=[
            pl.BlockSpec((n_rows, D_MODEL), lambda l, j: (0, 0)),
            pl.BlockSpec((1, D_MODEL, tn), lambda l, j: (l, 0, j)),
            pl.BlockSpec((1, 1, tn), lambda l, j: (l, 0, j)),
        ],
        out_specs=pl.BlockSpec((1, n_rows, tn), lambda l, j: (l, 0, j)),
        out_shape=jax.ShapeDtypeStruct((DEPTH, n_rows, 6 * D_MODEL), F32),
        compiler_params=_cparams("parallel", "parallel"),
        name="adaln_mod",
    )(c_all, w_ada, b_ada.reshape(DEPTH, 1, 6 * D_MODEL))


def _proj_kernel(x_ref, shift_ref, scale_ref, w_ref, us5_ref, hg_ref, gates_ref):
    u = x_ref[0] * (1.0 + scale_ref[0]) + shift_ref[0]
    ub = u.astype(BF16)
    c0, c1 = S5_WIDTH, S5_WIDTH + 4 * HG_WIDTH
    us5_ref[0] = jnp.dot(ub, w_ref[:, :c0], preferred_element_type=F32)
    hg_ref[0] = jnp.dot(ub, w_ref[:, c0:c1], preferred_element_type=F32)
    gates_ref[0] = jnp.dot(ub, w_ref[:, c1:], preferred_element_type=F32)


def _mod_spec(tm, per_row):
    if per_row:
        return pl.BlockSpec((1, tm, D_MODEL), lambda g, i: (g, i, 0))
    return pl.BlockSpec((1, 1, D_MODEL), lambda g, i: (g, 0, 0))


def _input_proj(x, shift, scale, w_in_bf, tm, per_row):
    n_g, n_r, _ = x.shape

    def row_spec(width):
        return pl.BlockSpec((1, tm, width), lambda g, i: (g, i, 0))

    return pl.pallas_call(
        _proj_kernel,
        grid=(n_g, n_r // tm),
        in_specs=[row_spec(D_MODEL), _mod_spec(tm, per_row), _mod_spec(tm, per_row),
                  pl.BlockSpec((D_MODEL, N_IN), lambda g, i: (0, 0))],
        out_specs=[row_spec(S5_WIDTH), row_spec(4 * HG_WIDTH), row_spec(2 * D_MODEL)],
        out_shape=[jax.ShapeDtypeStruct((n_g, n_r, S5_WIDTH), F32),
                   jax.ShapeDtypeStruct((n_g, n_r, 4 * HG_WIDTH), F32),
                   jax.ShapeDtypeStruct((n_g, n_r, 2 * D_MODEL), F32)],
        compiler_params=_cparams("parallel", "parallel"),
        name="input_proj",
    )(x, shift, scale, w_in_bf)


S5_SLABS = S5_FLAT // LANES
S5_SCAN_SLABS = 8
S5_SCAN_UNROLL = 4


def _split_bf16x3(x):
    p1 = x.astype(BF16)
    r1 = x - p1.astype(F32)
    p2 = r1.astype(BF16)
    p3 = (r1 - p2.astype(F32)).astype(BF16)
    return jnp.concatenate([p1, p2, p3], axis=1)


def _s5_kernel(u_ref, h0re_ref, h0im_ref, bb_ref, cc_ref, a_ref, d_ref, perm_ref, perm_t_ref,
               y_ref, hre_ref, him_ref, bu_scr, hs_scr, h_scr, *, tl):
    j = pl.program_id(1)
    rows = SUBLANES * tl
    u = u_ref[...].reshape(rows, S5_WIDTH)
    ub = u.astype(BF16)
    if tl > 1:
        ub = jnp.dot(perm_ref[...], ub, preferred_element_type=F32).astype(BF16)

    @pl.when(j == 0)
    def _():
        h_scr[0] = h0re_ref[...]
        h_scr[1] = h0im_ref[...]

    cols_per_in_tile = MXU_TILE * (S5_STATE // S5_GROUP)
    for n in range(2 * S5_FLAT // MXU_TILE):
        kt = (n * MXU_TILE % S5_FLAT) // cols_per_in_tile
        res = jnp.dot(ub[:, kt * MXU_TILE:(kt + 1) * MXU_TILE],
                      bb_ref[kt * MXU_TILE:(kt + 1) * MXU_TILE, n * MXU_TILE:(n + 1) * MXU_TILE],
                      preferred_element_type=F32)
        bu_scr[2 * n] = res[:, :LANES]
        bu_scr[2 * n + 1] = res[:, LANES:]

    for c0 in range(0, S5_SLABS, S5_SCAN_SLABS):
        slabs = range(c0, c0 + S5_SCAN_SLABS)
        a_re = [jnp.broadcast_to(a_ref[0:1, c * LANES:(c + 1) * LANES], (SUBLANES, LANES)) for c in slabs]
        a_im = [jnp.broadcast_to(a_ref[1:2, c * LANES:(c + 1) * LANES], (SUBLANES, LANES)) for c in slabs]

        def step(t, carry, slabs=slabs, a_re=a_re, a_im=a_im):
            rsel = pl.ds(pl.multiple_of(t * SUBLANES, SUBLANES), SUBLANES)
            new = []
            for q, c in enumerate(slabs):
                h_re, h_im = carry[2 * q], carry[2 * q + 1]
                n_re = a_re[q] * h_re - a_im[q] * h_im + bu_scr[c, rsel, :]
                n_im = a_re[q] * h_im + a_im[q] * h_re + bu_scr[S5_SLABS + c, rsel, :]
                hs_scr[c, rsel, :] = n_re
                hs_scr[S5_SLABS + c, rsel, :] = n_im
                new += [n_re, n_im]
            return tuple(new)

        init = []
        for c in slabs:
            init += [h_scr[0, :, c * LANES:(c + 1) * LANES], h_scr[1, :, c * LANES:(c + 1) * LANES]]
        fin = lax.fori_loop(0, tl, step, tuple(init), unroll=min(S5_SCAN_UNROLL, tl))
        for q, c in enumerate(slabs):
            h_scr[0, :, c * LANES:(c + 1) * LANES] = fin[2 * q]
            h_scr[1, :, c * LANES:(c + 1) * LANES] = fin[2 * q + 1]

    slabs_per_out = S5_SLABS * MXU_TILE // S5_WIDTH
    y_tiles = []
    for m in range(S5_WIDTH // MXU_TILE):
        acc = None
        for part in range(2):
            for c in range(m * slabs_per_out, (m + 1) * slabs_per_out, 2):
                s = part * S5_SLABS + c
                hb = jnp.concatenate([hs_scr[s], hs_scr[s + 1]], axis=1).astype(BF16)
                term = jnp.dot(hb, cc_ref[s * LANES:(s + 2) * LANES, m * MXU_TILE:(m + 1) * MXU_TILE],
                               preferred_element_type=F32)
                acc = term if acc is None else acc + term
        y_tiles.append(acc)
    ch = jnp.concatenate(y_tiles, axis=1)
    if tl > 1:
        back = jnp.dot(perm_t_ref[...], _split_bf16x3(ch), preferred_element_type=F32)
        ch = back[:, :S5_WIDTH] + back[:, S5_WIDTH:2 * S5_WIDTH] + back[:, 2 * S5_WIDTH:]
    y = jax.nn.gelu(ch + d_ref[...] * u)
    y_ref[...] = y.reshape(y_ref.shape)
    hre_ref[...] = h_scr[0]
    him_ref[...] = h_scr[1]


def _s5_branch(u, h0_re, h0_im, bb, cc, a, d, tl):
    n_b, n_l, _ = u.shape
    rows = SUBLANES * tl
    if n_l == 1:
        u = u.reshape(n_b // SUBLANES, SUBLANES, S5_WIDTH)
        u_spec = pl.BlockSpec((None, SUBLANES, S5_WIDTH), lambda g, j: (g, 0, 0))
    else:
        u_spec = pl.BlockSpec((SUBLANES, tl, S5_WIDTH), lambda g, j: (g, j, 0))
    st_spec = pl.BlockSpec((SUBLANES, S5_FLAT), lambda g, j: (g, 0))

    def const(shape):
        return pl.BlockSpec(shape, lambda g, j: (0, 0))

    src = (jnp.arange(rows) % SUBLANES) * tl + jnp.arange(rows) // SUBLANES
    perm = (src[:, None] == jnp.arange(rows)[None, :]).astype(BF16)
    slab_scratch = pltpu.VMEM((2 * S5_SLABS, rows, LANES), F32)
    return pl.pallas_call(
        functools.partial(_s5_kernel, tl=tl),
        grid=(n_b // SUBLANES, n_l // tl),
        in_specs=[u_spec, st_spec, st_spec, const((S5_WIDTH, 2 * S5_FLAT)),
                  const((2 * S5_FLAT, S5_WIDTH)), const((2, S5_FLAT)), const((1, S5_WIDTH)),
                  const((rows, rows)), const((rows, rows))],
        out_specs=[u_spec, st_spec, st_spec],
        out_shape=[jax.ShapeDtypeStruct(u.shape, F32),
                   jax.ShapeDtypeStruct((n_b, S5_FLAT), F32),
                   jax.ShapeDtypeStruct((n_b, S5_FLAT), F32)],
        scratch_shapes=[slab_scratch, slab_scratch, pltpu.VMEM((2, SUBLANES, S5_FLAT), F32)],
        compiler_params=_cparams("parallel", "arbitrary"),
        name="s5_branch",
    )(u, h0_re, h0_im, bb, cc, a, d, perm, perm.T)


def _s5_params(lam_re, lam_im, log_dt, b_re, b_im, c_re, c_im, d_skip):
    dt = jnp.exp(log_dt)[:, None]
    mag = jnp.exp(lam_re * dt)
    ang = lam_im * dt
    ab_re, ab_im = mag * jnp.cos(ang), mag * jnp.sin(ang)
    den = jnp.square(lam_re) + jnp.square(lam_im)
    nr, ni = ab_re - 1.0, ab_im
    zf_re = (nr * lam_re + ni * lam_im) / den
    zf_im = (ni * lam_re - nr * lam_im) / den
    bb_re = zf_re[..., None] * b_re - zf_im[..., None] * b_im
    bb_im = zf_re[..., None] * b_im + zf_im[..., None] * b_re
    eye = jnp.eye(S5_GROUPS, dtype=F32)

    def in_blockdiag(m):
        return jnp.einsum('gph,gk->ghkp', m, eye).reshape(S5_WIDTH, S5_FLAT)

    def out_blockdiag(m):
        return jnp.einsum('ghp,gk->gpkh', m, eye).reshape(S5_FLAT, S5_WIDTH)

    bb = jnp.concatenate([in_blockdiag(bb_re), in_blockdiag(bb_im)], axis=1).astype(BF16)
    cc = jnp.concatenate([out_blockdiag(c_re), -out_blockdiag(c_im)], axis=0).astype(BF16)
    a = jnp.stack([ab_re.reshape(S5_FLAT), ab_im.reshape(S5_FLAT)])
    return bb, cc, a, d_skip.reshape(1, S5_WIDTH)


def _hgrn_kernel(hg_ref, s0_ref, lb_ref, gw_ref, y_ref, sout_ref, st_scr,
                 *, tl, valid_len, first_layer):
    j = pl.program_id(1)
    n_t = pl.num_programs(1)

    @pl.when(j == 0)
    def _():
        for h in range(HG_HEADS):
            st_scr[h] = s0_ref[0, 0, h].T

    row = lax.broadcasted_iota(jnp.int32, (HG_CHUNK, HG_WIDTH), 0)
    row_col = lax.broadcasted_iota(jnp.int32, (HG_CHUNK, 1), 0)
    lb = lb_ref[...]
    gw = gw_ref[...]

    def chunk_body(c, carry):
        r0 = pl.multiple_of(c * HG_CHUNK, HG_CHUNK)
        blk = hg_ref[0, pl.ds(r0, HG_CHUNK), :]
        q = blk[:, 0:HG_WIDTH]
        f_pre = blk[:, HG_WIDTH:2 * HG_WIDTH]
        v = blk[:, 2 * HG_WIDTH:3 * HG_WIDTH]
        g_out = blk[:, 3 * HG_WIDTH:4 * HG_WIDTH]
        if first_layer:
            log_f = jnp.minimum(f_pre, 0.0) - jnp.log1p(jnp.exp(-jnp.abs(f_pre)))
            k = jax.nn.sigmoid(-f_pre)
        else:
            log_f = jnp.log(lb + (1.0 - lb) * jax.nn.sigmoid(f_pre))
            k = (1.0 - lb) * jax.nn.sigmoid(-f_pre)
        if valid_len is not None:
            live = (j * tl + r0 + row) < valid_len
            log_f = jnp.where(live, log_f, 0.0)
            k = jnp.where(live, k, 0.0)
        b = log_f
        sh = 1
        while sh < HG_CHUNK:
            b = b + jnp.where(row >= sh, pltpu.roll(b, sh, 0), 0.0)
            sh *= 2

        n_rb = HG_CHUNK // SUBLANES
        for h in range(HG_HEADS):
            hs = slice(h * HG_HEAD_DIM, (h + 1) * HG_HEAD_DIM)
            qh, kh, vh, bh = q[:, hs], k[:, hs], v[:, hs], b[:, hs]
            b_last = bh[HG_CHUNK - 1:HG_CHUNK, :]
            st = st_scr[h]
            o = lax.dot_general((qh * jnp.exp(bh)).astype(BF16), st.astype(BF16),
                                (((1,), (1,)), ((), ())), preferred_element_type=F32)
            o_rb = []
            for i in range(n_rb):
                rs = slice(i * SUBLANES, (i + 1) * SUBLANES)
                q_i, b_i, acc = qh[rs], bh[rs], o[rs]
                for s in range((i + 1) * SUBLANES):
                    diff = b_i - bh[s:s + 1, :]
                    if s >= i * SUBLANES:
                        diff = jnp.minimum(diff, 0.0)
                    att = jnp.sum(q_i * kh[s:s + 1, :] * jnp.exp(diff), axis=-1, keepdims=True)
                    if s >= i * SUBLANES:
                        att = jnp.where(row_col[rs] >= s, att, 0.0)
                    acc = acc + att * vh[s:s + 1, :]
                o_rb.append(acc)
            o = jnp.concatenate(o_rb, axis=0)
            k_dec = kh * jnp.exp(b_last - bh)
            upd = lax.dot_general(vh.astype(BF16), k_dec.astype(BF16),
                                  (((0,), (0,)), ((), ())), preferred_element_type=F32)
            st_scr[h] = st * jnp.exp(b_last) + upd
            o = o * lax.rsqrt(jnp.mean(o * o, axis=-1, keepdims=True) + RMS_EPS) * gw
            gh = g_out[:, hs]
            y_ref[0, pl.ds(r0, HG_CHUNK), hs] = o * (gh * jax.nn.sigmoid(gh))
        return carry

    lax.fori_loop(0, tl // HG_CHUNK, chunk_body, 0, unroll=min(HG_UNROLL, tl // HG_CHUNK))

    @pl.when(j == n_t - 1)
    def _():
        for h in range(HG_HEADS):
            sout_ref[0, h] = st_scr[h].T


def _hgrn_branch(hg, s0_layers, layer, lb, gnorm_w, tl, valid_len, first_layer):
    n_b, n_l, _ = hg.shape
    st_shape = (HG_HEADS, HG_HEAD_DIM, HG_HEAD_DIM)
    return pl.pallas_call(
        functools.partial(_hgrn_kernel, tl=tl, valid_len=valid_len, first_layer=first_layer),
        grid=(n_b, n_l // tl),
        in_specs=[pl.BlockSpec((1, tl, 4 * HG_WIDTH), lambda b, j: (b, j, 0)),
                  pl.BlockSpec((1, 1) + st_shape, lambda b, j: (layer, b, 0, 0, 0)),
                  pl.BlockSpec((1, HG_WIDTH), lambda b, j: (0, 0)),
                  pl.BlockSpec((1, HG_HEAD_DIM), lambda b, j: (0, 0))],
        out_specs=[pl.BlockSpec((1, tl, HG_WIDTH), lambda b, j: (b, j, 0)),
                   pl.BlockSpec((1,) + st_shape, lambda b, j: (b, 0, 0, 0))],
        out_shape=[jax.ShapeDtypeStruct((n_b, n_l, HG_WIDTH), F32),
                   jax.ShapeDtypeStruct((n_b,) + st_shape, F32)],
        scratch_shapes=[pltpu.VMEM(st_shape, F32)],
        compiler_params=_cparams("parallel", "arbitrary"),
        name="hgrn_branch",
    )(hg, s0_layers, lb.reshape(1, HG_WIDTH), gnorm_w.reshape(1, HG_HEAD_DIM))


def _mix_kernel(ys5_ref, yhg_ref, gates_ref, x_ref, gate1_ref, shift2_ref, scale2_ref,
                wglu_ref, bglu_ref, wus_ref, wuh_ref, wout_ref, g_ref, b_ref, rw_ref, rb_ref,
                x1_ref, u2_ref, topi_ref, topw_ref):
    ys = ys5_ref[0]
    glu = ys * jax.nn.sigmoid(
        jnp.dot(ys.astype(BF16), wglu_ref[...], preferred_element_type=F32) + bglu_ref[...])
    gates = jax.nn.sigmoid(gates_ref[0])
    merged = (gates[:, :D_MODEL] * jnp.dot(glu.astype(BF16), wus_ref[...], preferred_element_type=F32)
              + gates[:, D_MODEL:] * jnp.dot(yhg_ref[0].astype(BF16), wuh_ref[...],
                                             preferred_element_type=F32))
    mix = jnp.dot(merged.astype(BF16), wout_ref[...], preferred_element_type=F32)
    x1 = _layer_norm(DEEPNORM_ALPHA * x_ref[0] + gate1_ref[0] * mix, g_ref[...], b_ref[...])
    x1_ref[0] = x1
    u2 = x1 * (1.0 + scale2_ref[0]) + shift2_ref[0]
    u2_ref[0] = u2
    u_hi = u2.astype(BF16)
    u_lo = (u2 - u_hi.astype(F32)).astype(BF16)
    hh_hl = jnp.dot(u_hi, rw_ref[...], preferred_element_type=F32)
    lh = jnp.dot(u_lo, rw_ref[:, :LANES], preferred_element_type=F32)
    logits = (hh_hl[:, :LANES] + hh_hl[:, LANES:]) + lh + rb_ref[...]
    lane = lax.broadcasted_iota(jnp.int32, logits.shape, 1)
    lane_f = lane.astype(F32)
    vals, idxs = [], []
    for _ in range(TOP_K):
        m = jnp.max(logits, axis=-1, keepdims=True)
        idx = jnp.min(jnp.where(logits == m, lane_f, float(LANES)), axis=-1,
                      keepdims=True).astype(jnp.int32)
        vals.append(m)
        idxs.append(idx)
        logits = jnp.where(lane == idx, NEG_BIG, logits)
    exps = [jnp.exp(val - vals[0]) for val in vals]
    denom = exps[0] + exps[1] + exps[2] + exps[3]
    topi = jnp.full(lane.shape, -1, jnp.int32)
    topw = jnp.zeros(lane.shape, F32)
    for kk in range(TOP_K):
        topi = jnp.where(lane == kk, idxs[kk], topi)
        topw = jnp.where(lane == kk, exps[kk] / denom, topw)
    topi_ref[0] = topi
    topw_ref[0] = topw


def _mix(ys5, yhg, gates, x, gate1, shift2, scale2, lw, tm, per_row):
    n_g, n_r, _ = x.shape

    def row_spec(width):
        return pl.BlockSpec((1, tm, width), lambda g, i: (g, i, 0))

    def const(shape):
        return pl.BlockSpec(shape, lambda g, i: (0, 0))

    ms = _mod_spec(tm, per_row)
    return pl.pallas_call(
        _mix_kernel,
        grid=(n_g, n_r // tm),
        in_specs=[row_spec(S5_WIDTH), row_spec(HG_WIDTH), row_spec(2 * D_MODEL), row_spec(D_MODEL),
                  ms, ms, ms,
                  const((S5_WIDTH, S5_WIDTH)), const((1, S5_WIDTH)),
                  const((S5_WIDTH, D_MODEL)), const((HG_WIDTH, D_MODEL)),
                  const((D_MODEL, D_MODEL)), const((1, D_MODEL)), const((1, D_MODEL)),
                  const((D_MODEL, 2 * LANES)), const((1, LANES))],
        out_specs=[row_spec(D_MODEL), row_spec(D_MODEL), row_spec(LANES), row_spec(LANES)],
        out_shape=[jax.ShapeDtypeStruct((n_g, n_r, D_MODEL), F32),
                   jax.ShapeDtypeStruct((n_g, n_r, D_MODEL), F32),
                   jax.ShapeDtypeStruct((n_g, n_r, LANES), jnp.int32),
                   jax.ShapeDtypeStruct((n_g, n_r, LANES), F32)],
        compiler_params=_cparams("parallel", "parallel"),
        name="branch_mix",
    )(ys5, yhg, gates, x, gate1, shift2, scale2,
      lw['w_glu'], lw['b_glu'], lw['w_up_s5'], lw['w_up_hg'], lw['w_out'],
      lw['ln1_g'], lw['ln1_b'], lw['router_w'], lw['router_b'])


RANK_TILE = 384


def _expert_onehot(top_idx, kk):
    lane = lax.broadcasted_iota(jnp.int32, top_idx.shape, 1)
    return (top_idx[:, kk:kk + 1] == lane).astype(F32)


def _count_kernel(topi_ref, cnt_ref):
    @pl.when(pl.program_id(0) == 0)
    def _():
        cnt_ref[...] = jnp.zeros(cnt_ref.shape, F32)

    top_idx = topi_ref[...]
    acc = cnt_ref[...]
    for kk in range(TOP_K):
        acc = acc + jnp.sum(_expert_onehot(top_idx, kk), axis=0, keepdims=True)
    cnt_ref[...] = acc


def _rank_kernel(topi_ref, start_ref, dest_ref, off_scr):
    @pl.when(pl.program_id(0) == 0)
    def _():
        off_scr[...] = start_ref[...]

    top_idx = topi_ref[...]
    tm = top_idx.shape[0]
    lane = lax.broadcasted_iota(jnp.int32, top_idx.shape, 1)
    earlier = (lax.broadcasted_iota(jnp.int32, (tm, tm), 1)
               < lax.broadcasted_iota(jnp.int32, (tm, tm), 0)).astype(BF16)
    off = off_scr[...]
    dest = jnp.zeros(top_idx.shape, jnp.int32)
    for kk in range(TOP_K):
        onehot = _expert_onehot(top_idx, kk)
        before = jnp.dot(earlier, onehot.astype(BF16), preferred_element_type=F32)
        slot = jnp.sum(onehot * (before + off), axis=-1, keepdims=True)
        dest = jnp.where(lane == kk, slot.astype(jnp.int32), dest)
        off = off + jnp.sum(onehot, axis=0, keepdims=True)
    dest_ref[...] = dest
    off_scr[...] = off


def _slot_tables(top_idx_all):
    n_tok = top_idx_all.shape[0]
    n_tiles = n_tok // RANK_TILE
    tile_spec = pl.BlockSpec((RANK_TILE, LANES), lambda i: (i, 0))
    lane_spec = pl.BlockSpec((1, LANES), lambda i: (0, 0))
    counts = pl.pallas_call(
        _count_kernel, grid=(n_tiles,), in_specs=[tile_spec], out_specs=lane_spec,
        out_shape=jax.ShapeDtypeStruct((1, LANES), F32),
        compiler_params=_cparams("arbitrary"), name="expert_count",
    )(top_idx_all)
    counts = counts[0, :N_EXPERTS].astype(jnp.int32)
    padded = (counts + MOE_BLOCK - 1) // MOE_BLOCK * MOE_BLOCK
    pad_end = jnp.cumsum(padded)
    pad_start = pad_end - padded
    start_row = jnp.zeros((1, LANES), F32).at[0, :N_EXPERTS].set(pad_start.astype(F32))
    dest = pl.pallas_call(
        _rank_kernel, grid=(n_tiles,), in_specs=[tile_spec, lane_spec], out_specs=tile_spec,
        out_shape=jax.ShapeDtypeStruct((n_tok, LANES), jnp.int32),
        scratch_shapes=[pltpu.VMEM((1, LANES), F32)],
        compiler_params=_cparams("arbitrary"), name="expert_rank",
    )(top_idx_all, start_row)
    n_blocks = -(-(n_tok * TOP_K) // MOE_BLOCK) + N_EXPERTS
    block_lo = jnp.arange(n_blocks, dtype=jnp.int32) * MOE_BLOCK
    block_e = jnp.minimum(jnp.sum((pad_end[None, :] <= block_lo[:, None]).astype(jnp.int32), axis=1),
                          N_EXPERTS - 1).astype(jnp.int32)
    n_used = (pad_end[-1] // MOE_BLOCK).astype(jnp.int32).reshape(1)
    return dest[:, :TOP_K].reshape(-1), pad_end.astype(jnp.int32), block_e, n_used, n_blocks


def _dispatch_kernel(dest_ref, pad_end_ref, n_used_ref, up_ref, us_ref, xb_hbm, zero_scr, sem,
                     *, tm_p, n_tiles_p, n_blocks):
    i = pl.program_id(0)

    def zero_block(row0):
        return pltpu.make_async_copy(zero_scr, xb_hbm.at[pl.ds(row0, MOE_BLOCK)], sem)

    @pl.when(i == 0)
    def _():
        zero_scr[...] = jnp.zeros(zero_scr.shape, F32)

        def pad_block(e):
            start = pad_end_ref[e - 1] if e else 0
            end = pad_end_ref[e]
            return end > start, pl.multiple_of(jnp.maximum(end - MOE_BLOCK, 0), MOE_BLOCK)

        def tail_start(blk, carry):
            zero_block(pl.multiple_of(blk * MOE_BLOCK, MOE_BLOCK)).start()
            return carry

        def tail_wait(blk, carry):
            zero_block(pl.multiple_of(blk * MOE_BLOCK, MOE_BLOCK)).wait()
            return carry

        for e in range(N_EXPERTS):
            live, row0 = pad_block(e)
            pl.when(live)(lambda row0=row0: zero_block(row0).start())
        lax.fori_loop(n_used_ref[0], n_blocks, tail_start, 0)
        for e in range(N_EXPERTS):
            live, row0 = pad_block(e)
            pl.when(live)(lambda row0=row0: zero_block(row0).wait())
        lax.fori_loop(n_used_ref[0], n_blocks, tail_wait, 0)

    def scatter(src_ref, n_rows, first_tok):
        def issue(r, carry):
            for kk in range(TOP_K):
                slot = dest_ref[(first_tok + r) * TOP_K + kk]
                pltpu.make_async_copy(src_ref.at[pl.ds(r, 1)], xb_hbm.at[pl.ds(slot, 1)], sem).start()
            return carry

        lax.fori_loop(0, n_rows, issue, 0, unroll=DMA_ISSUE_UNROLL)
        for kk in range(TOP_K):
            pltpu.make_async_copy(src_ref, xb_hbm.at[pl.ds(0, n_rows)], sem).wait()

    @pl.when(i < n_tiles_p)
    def _():
        scatter(up_ref, tm_p, i * tm_p)

    @pl.when(i == n_tiles_p)
    def _():
        scatter(us_ref, us_ref.shape[0], n_tiles_p * tm_p)


def _dispatch(dest, pad_end, n_used, u_prompt, u_sample, n_blocks, tm_p):
    n_tiles_p = u_prompt.shape[0] // tm_p
    n_s = u_sample.shape[0]
    grid_spec = pltpu.PrefetchScalarGridSpec(
        num_scalar_prefetch=3, grid=(n_tiles_p + 1,),
        in_specs=[pl.BlockSpec((tm_p, D_MODEL), lambda i, d, p, nu: (jnp.minimum(i, n_tiles_p - 1), 0)),
                  pl.BlockSpec((n_s, D_MODEL), lambda i, d, p, nu: (0, 0))],
        out_specs=pl.BlockSpec(memory_space=pl.ANY),
        scratch_shapes=[pltpu.VMEM((MOE_BLOCK, D_MODEL), F32), pltpu.SemaphoreType.DMA(())])
    return pl.pallas_call(
        functools.partial(_dispatch_kernel, tm_p=tm_p, n_tiles_p=n_tiles_p, n_blocks=n_blocks),
        grid_spec=grid_spec,
        out_shape=jax.ShapeDtypeStruct((n_blocks * MOE_BLOCK, D_MODEL), F32),
        compiler_params=_cparams("arbitrary"),
        name="moe_dispatch",
    )(dest, pad_end, n_used, u_prompt, u_sample)


def _moe_kernel(block_e_ref, n_used_ref, x_ref, wgu_ref, bgu_ref, wd_ref, bd_ref, y_ref,
                wgu_bf, wd_bf):
    i = pl.program_id(0)
    n_used = n_used_ref[0]
    e = block_e_ref[i]
    e_prev = block_e_ref[jnp.maximum(i - 1, 0)]

    @pl.when(jnp.logical_and(i < n_used, jnp.logical_or(i == 0, e != e_prev)))
    def _():
        wgu_bf[...] = wgu_ref[0, 0].astype(BF16)
        wd_bf[...] = wd_ref[0, 0].astype(BF16)

    @pl.when(i < n_used)
    def _():
        h = jnp.dot(x_ref[...].astype(BF16), wgu_bf[...], preferred_element_type=F32) + bgu_ref[0, 0]
        gate = jnp.minimum(h[:, :D_EXPERT], SWIGLU_LIMIT)
        up = jnp.clip(h[:, D_EXPERT:], -SWIGLU_LIMIT, SWIGLU_LIMIT)
        act = gate * jax.nn.sigmoid(SWIGLU_ALPHA * gate) * (up + 1.0)
        y_ref[...] = jnp.dot(act.astype(BF16), wd_bf[...], preferred_element_type=F32) + bd_ref[0, 0]

    @pl.when(i >= n_used)
    def _():
        y_ref[...] = jnp.zeros(y_ref.shape, F32)


def _moe_experts(xb, block_e, n_used, layer, w_gate_up, b_gate_up, w_down, b_down):
    n_blocks = block_e.shape[0]

    def expert_spec(shape):
        return pl.BlockSpec((1, 1) + shape, lambda i, be, nu: (layer, be[i], 0, 0))

    grid_spec = pltpu.PrefetchScalarGridSpec(
        num_scalar_prefetch=2,
        grid=(n_blocks,),
        in_specs=[pl.BlockSpec((MOE_BLOCK, D_MODEL), lambda i, be, nu: (jnp.minimum(i, nu[0] - 1), 0)),
                  expert_spec((D_MODEL, 2 * D_EXPERT)), expert_spec((1, 2 * D_EXPERT)),
                  expert_spec((D_EXPERT, D_MODEL)), expert_spec((1, D_MODEL))],
        out_specs=pl.BlockSpec((MOE_BLOCK, D_MODEL), lambda i, be, nu: (i, 0)),
        scratch_shapes=[pltpu.VMEM((D_MODEL, 2 * D_EXPERT), BF16),
                        pltpu.VMEM((D_EXPERT, D_MODEL), BF16)],
    )
    return pl.pallas_call(
        _moe_kernel,
        grid_spec=grid_spec,
        out_shape=jax.ShapeDtypeStruct((n_blocks * MOE_BLOCK, D_MODEL), F32),
        compiler_params=_cparams("arbitrary"),
        name="moe_experts",
    )(block_e, n_used, xb, w_gate_up,
      b_gate_up.reshape(DEPTH, N_EXPERTS, 1, 2 * D_EXPERT),
      w_down, b_down.reshape(DEPTH, N_EXPERTS, 1, D_MODEL))


def _combine_kernel(dest_ref, yb_hbm, x1_ref, topw_ref, gate2_ref, g_ref, b_ref, o_ref, ybuf, sem,
                    *, tm, tok_offset):
    g = pl.program_id(0)
    i = pl.program_id(1)
    base = (tok_offset + (g * pl.num_programs(1) + i) * tm) * TOP_K

    def issue(r, carry):
        for kk in range(TOP_K):
            slot = dest_ref[base + r * TOP_K + kk]
            pltpu.make_async_copy(yb_hbm.at[pl.ds(slot, 1)], ybuf.at[kk, pl.ds(r, 1)], sem).start()
        return carry

    lax.fori_loop(0, tm, issue, 0, unroll=DMA_ISSUE_UNROLL)
    for kk in range(TOP_K):
        pltpu.make_async_copy(yb_hbm.at[pl.ds(0, tm)], ybuf.at[kk], sem).wait()
    topw = topw_ref[0]
    ffn = ((topw[:, 0:1] * ybuf[0] + topw[:, 1:2] * ybuf[1])
           + (topw[:, 2:3] * ybuf[2] + topw[:, 3:4] * ybuf[3]))
    o_ref[0] = _layer_norm(DEEPNORM_ALPHA * x1_ref[0] + gate2_ref[0] * ffn, g_ref[...], b_ref[...])


def _combine(dest, yb, x1, topw, gate2, ln_g, ln_b, tm, per_row, tok_offset):
    n_g, n_r, _ = x1.shape
    ms = (pl.BlockSpec((1, tm, D_MODEL), lambda g, i, d: (g, i, 0)) if per_row
          else pl.BlockSpec((1, 1, D_MODEL), lambda g, i, d: (g, 0, 0)))
    grid_spec = pltpu.PrefetchScalarGridSpec(
        num_scalar_prefetch=1,
        grid=(n_g, n_r // tm),
        in_specs=[pl.BlockSpec(memory_space=pl.ANY),
                  pl.BlockSpec((1, tm, D_MODEL), lambda g, i, d: (g, i, 0)),
                  pl.BlockSpec((1, tm, LANES), lambda g, i, d: (g, i, 0)),
                  ms,
                  pl.BlockSpec((1, D_MODEL), lambda g, i, d: (0, 0)),
                  pl.BlockSpec((1, D_MODEL), lambda g, i, d: (0, 0))],
        out_specs=pl.BlockSpec((1, tm, D_MODEL), lambda g, i, d: (g, i, 0)),
        scratch_shapes=[pltpu.VMEM((TOP_K, tm, D_MODEL), F32), pltpu.SemaphoreType.DMA(())],
    )
    return pl.pallas_call(
        functools.partial(_combine_kernel, tm=tm, tok_offset=tok_offset),
        grid_spec=grid_spec,
        out_shape=jax.ShapeDtypeStruct(x1.shape, F32),
        compiler_params=_cparams("arbitrary", "arbitrary"),
        name="moe_combine",
    )(dest, yb, x1, topw, gate2, ln_g, ln_b)


def kernel(x_prompt, x_sample, state_s5_re, state_s5_im, state_hgrn, c_prompt, c_sample,
           w_ada, b_ada, w_in, s5_lambda_re, s5_lambda_im, s5_log_dt, s5_b_re, s5_b_im,
           s5_c_re, s5_c_im, s5_d, w_glu, b_glu, w_up_s5, hgrn_lb_logits, hgrn_gnorm_w,
           w_up_hg, w_out, ln1_g, ln1_b, router_w, router_b, w_gate_up, b_gate_up,
           w_down, b_down, ln2_g, ln2_b):
    n_p, seq, _ = x_prompt.shape
    n_s = x_sample.shape[0]
    n_tok_p = n_p * seq

    lb_p = jax.nn.softmax(hgrn_lb_logits.astype(F32), axis=0)
    lb_all = jnp.cumsum(lb_p, axis=0) - lb_p[0]

    mod = _modulation(jnp.concatenate([c_prompt, c_sample], axis=0), w_ada, b_ada)

    def mods(layer, which):
        sl = slice(which * D_MODEL, (which + 1) * D_MODEL)
        return (mod[layer, :n_p, sl].reshape(n_p, 1, D_MODEL),
                mod[layer, n_p:, sl].reshape(1, n_s, D_MODEL))

    xp = x_prompt
    xs = x_sample.reshape(1, n_s, D_MODEL)
    zero_s5 = jnp.zeros((n_p, S5_FLAT), F32)
    zero_hg = jnp.zeros((1, n_p, HG_HEADS, HG_HEAD_DIM, HG_HEAD_DIM), F32)
    outs = {k: [] for k in ('p_re', 'p_im', 'p_hg', 's_re', 's_im', 's_hg')}

    for l in range(DEPTH):
        shift1, scale1, gate1, shift2, scale2, gate2 = [mods(l, w) for w in range(6)]
        w_in_bf = w_in[l].astype(BF16)
        bb, cc, a, d = _s5_params(s5_lambda_re[l], s5_lambda_im[l], s5_log_dt[l], s5_b_re[l],
                                  s5_b_im[l], s5_c_re[l], s5_c_im[l], s5_d[l])
        router_w_pad = jnp.zeros((D_MODEL, LANES), F32).at[:, :N_EXPERTS].set(router_w[l])
        router_w_hi = router_w_pad.astype(BF16)
        router_w_lo = (router_w_pad - router_w_hi.astype(F32)).astype(BF16)
        router_b_pad = jnp.full((1, LANES), NEG_BIG, F32).at[0, :N_EXPERTS].set(router_b[l])
        lw = {'w_glu': w_glu[l].astype(BF16), 'b_glu': b_glu[l].reshape(1, S5_WIDTH),
              'w_up_s5': w_up_s5[l].astype(BF16), 'w_up_hg': w_up_hg[l].astype(BF16),
              'w_out': w_out[l].astype(BF16), 'ln1_g': ln1_g[l].reshape(1, D_MODEL),
              'ln1_b': ln1_b[l].reshape(1, D_MODEL), 'router_w': jnp.concatenate([router_w_hi, router_w_lo], axis=1),
              'router_b': router_b_pad}
        first = l == 0

        us5_p, hg_p, gates_p = _input_proj(xp, shift1[0], scale1[0], w_in_bf, 256, False)
        ys5_p, pre, pim = _s5_branch(us5_p, zero_s5, zero_s5, bb, cc, a, d, 64)
        yhg_p, phg = _hgrn_branch(hg_p, zero_hg, 0, lb_all[l], hgrn_gnorm_w[l], 128, None, first)
        x1_p, u2_p, ti_p, tw_p = _mix(ys5_p, yhg_p, gates_p, xp, gate1[0], shift2[0], scale2[0],
                                      lw, 256, False)
        us5_s, hg_s, gates_s = _input_proj(xs, shift1[1], scale1[1], w_in_bf, n_s, True)
        ys5_s, sre, sim = _s5_branch(us5_s.reshape(n_s, 1, S5_WIDTH),
                                     state_s5_re[l].reshape(n_s, S5_FLAT),
                                     state_s5_im[l].reshape(n_s, S5_FLAT), bb, cc, a, d, 1)
        hg_s_pad = jnp.pad(hg_s.reshape(n_s, 1, 4 * HG_WIDTH), ((0, 0), (0, HG_CHUNK - 1), (0, 0)))
        yhg_s, shg = _hgrn_branch(hg_s_pad, state_hgrn, l, lb_all[l], hgrn_gnorm_w[l],
                                  HG_CHUNK, 1, first)
        x1_s, u2_s, ti_s, tw_s = _mix(ys5_s.reshape(1, n_s, S5_WIDTH),
                                      yhg_s[:, 0, :].reshape(1, n_s, HG_WIDTH), gates_s, xs,
                                      gate1[1], shift2[1], scale2[1], lw, n_s, True)

        top_idx_all = jnp.concatenate([ti_p.reshape(n_tok_p, LANES), ti_s.reshape(n_s, LANES)], axis=0)
        dest, pad_end, block_e, n_used, n_blocks = _slot_tables(top_idx_all)
        xb = _dispatch(dest, pad_end, n_used, u2_p.reshape(n_tok_p, D_MODEL),
                       u2_s.reshape(n_s, D_MODEL), n_blocks, 512)
        yb = _moe_experts(xb, block_e, n_used, l, w_gate_up, b_gate_up, w_down, b_down)
        g2, b2 = ln2_g[l].reshape(1, D_MODEL), ln2_b[l].reshape(1, D_MODEL)
        xp = _combine(dest, yb, x1_p, tw_p, gate2[0], g2, b2, 256, False, 0)
        xs = _combine(dest, yb, x1_s, tw_s, gate2[1], g2, b2, n_s, True, n_tok_p)

        outs['p_re'].append(pre.reshape(n_p, S5_GROUPS, S5_STATE))
        outs['p_im'].append(pim.reshape(n_p, S5_GROUPS, S5_STATE))
        outs['p_hg'].append(phg)
        outs['s_re'].append(sre.reshape(n_s, S5_GROUPS, S5_STATE))
        outs['s_im'].append(sim.reshape(n_s, S5_GROUPS, S5_STATE))
        outs['s_hg'].append(shg)

    return (xp, xs.reshape(n_s, 1, D_MODEL),
            jnp.stack(outs['p_re']), jnp.stack(outs['p_im']), jnp.stack(outs['p_hg']),
            jnp.stack(outs['s_re']), jnp.stack(outs['s_im']), jnp.stack(outs['s_hg']))
```

```python
import functools

import jax
import jax.numpy as jnp
from jax import lax
from jax.experimental import pallas as pl
from jax.experimental.pallas import tpu as pltpu

F32 = jnp.float32
BF16 = jnp.bfloat16

LANES = 128
SUBLANES = 8
MXU_TILE = 256
VMEM_LIMIT = 56 * 1024 * 1024
NEG_BIG = -1e30

D_MODEL = 1024
DEPTH = 2
S5_WIDTH = 512
S5_GROUP = 16
S5_GROUPS = 32
S5_STATE = 64
S5_FLAT = S5_GROUPS * S5_STATE
HG_WIDTH = 512
HG_HEAD_DIM = 128
HG_HEADS = 4
HG_CHUNK = 16
HG_UNROLL = 2
RMS_EPS = 1e-6
N_IN = S5_WIDTH + 4 * HG_WIDTH + 2 * D_MODEL
N_EXPERTS = 32
TOP_K = 4
D_EXPERT = D_MODEL
SWIGLU_LIMIT = 7.0
SWIGLU_ALPHA = 1.702
MOE_BLOCK = 512
DMA_ISSUE_UNROLL = 4
DEEPNORM_ALPHA = (2 * DEPTH) ** 0.25
LN_EPS = 1e-5


def _cparams(*sem):
    return pltpu.CompilerParams(dimension_semantics=sem, vmem_limit_bytes=VMEM_LIMIT)


def _layer_norm(x, g, b):
    mu = jnp.mean(x, axis=-1, keepdims=True)
    xc = x - mu
    var = jnp.mean(xc * xc, axis=-1, keepdims=True)
    return xc * lax.rsqrt(var + LN_EPS) * g + b


def _mod_kernel(c_ref, w_ref, b_ref, o_ref):
    c = c_ref[...]
    s = c * jax.nn.sigmoid(c)
    o_ref[0] = jnp.dot(s.astype(BF16), w_ref[0].astype(BF16),
                       preferred_element_type=F32) + b_ref[0]


def _modulation(c_all, w_ada, b_ada):
    n_rows = c_all.shape[0]
    tn = 1536
    return pl.pallas_call(
        _mod_kernel,
        grid=(DEPTH, 6 * D_MODEL // tn),
        in_specs=[
            pl.BlockSpec((n_rows, D_MODEL), lambda l, j: (0, 0)),
            pl.BlockSpec((1, D_MODEL, tn), lambda l, j: (l, 0, j)),
            pl.BlockSpec((1, 1, tn), lambda l, j: (l, 0, j)),
        ],
        out_specs=pl.BlockSpec((1, n_rows, tn), lambda l, j: (l, 0, j)),
        out_shape=jax.ShapeDtypeStruct((DEPTH, n_rows, 6 * D_MODEL), F32),
        compiler_params=_cparams("parallel", "parallel"),
        name="adaln_mod",
    )(c_all, w_ada, b_ada.reshape(DEPTH, 1, 6 * D_MODEL))


def _proj_kernel(x_ref, shift_ref, scale_ref, w_ref, us5_ref, hg_ref, gates_ref):
    u = x_ref[0] * (1.0 + scale_ref[0]) + shift_ref[0]
    ub = u.astype(BF16)
    c0, c1 = S5_WIDTH, S5_WIDTH + 4 * HG_WIDTH
    us5_ref[0] = jnp.dot(ub, w_ref[:, :c0], preferred_element_type=F32)
    hg_ref[0] = jnp.dot(ub, w_ref[:, c0:c1], preferred_element_type=F32)
    gates_ref[0] = jnp.dot(ub, w_ref[:, c1:], preferred_element_type=F32)


def _mod_spec(tm, per_row):
    if per_row:
        return pl.BlockSpec((1, tm, D_MODEL), lambda g, i: (g, i, 0))
    return pl.BlockSpec((1, 1, D_MODEL), lambda g, i: (g, 0, 0))


def _input_proj(x, shift, scale, w_in_bf, tm, per_row):
    n_g, n_r, _ = x.shape

    def row_spec(width):
        return pl.BlockSpec((1, tm, width), lambda g, i: (g, i, 0))

    return pl.pallas_call(
        _proj_kernel,
        grid=(n_g, n_r // tm),
        in_specs=[row_spec(D_MODEL), _mod_spec(tm, per_row), _mod_spec(tm, per_row),
                  pl.BlockSpec((D_MODEL, N_IN), lambda g, i: (0, 0))],
        out_specs=[row_spec(S5_WIDTH), row_spec(4 * HG_WIDTH), row_spec(2 * D_MODEL)],
        out_shape=[jax.ShapeDtypeStruct((n_g, n_r, S5_WIDTH), F32),
                   jax.ShapeDtypeStruct((n_g, n_r, 4 * HG_WIDTH), F32),
                   jax.ShapeDtypeStruct((n_g, n_r, 2 * D_MODEL), F32)],
        compiler_params=_cparams("parallel", "parallel"),
        name="input_proj",
    )(x, shift, scale, w_in_bf)


S5_SLABS = S5_FLAT // LANES
S5_SCAN_SLABS = 8
S5_SCAN_UNROLL = 4


def _split_bf16x3(x):
    p1 = x.astype(BF16)
    r1 = x - p1.astype(F32)
    p2 = r1.astype(BF16)
    p3 = (r1 - p2.astype(F32)).astype(BF16)
    return jnp.concatenate([p1, p2, p3], axis=1)


def _s5_kernel(u_ref, h0re_ref, h0im_ref, bb_ref, cc_ref, a_ref, d_ref, perm_ref, perm_t_ref,
               y_ref, hre_ref, him_ref, bu_scr, hs_scr, h_scr, *, tl):
    j = pl.program_id(1)
    rows = SUBLANES * tl
    u = u_ref[...].reshape(rows, S5_WIDTH)
    ub = u.astype(BF16)
    if tl > 1:
        ub = jnp.dot(perm_ref[...], ub, preferred_element_type=F32).astype(BF16)

    @pl.when(j == 0)
    def _():
        h_scr[0] = h0re_ref[...]
        h_scr[1] = h0im_ref[...]

    cols_per_in_tile = MXU_TILE * (S5_STATE // S5_GROUP)
    for n in range(2 * S5_FLAT // MXU_TILE):
        kt = (n * MXU_TILE % S5_FLAT) // cols_per_in_tile
        res = jnp.dot(ub[:, kt * MXU_TILE:(kt + 1) * MXU_TILE],
                      bb_ref[kt * MXU_TILE:(kt + 1) * MXU_TILE, n * MXU_TILE:(n + 1) * MXU_TILE],
                      preferred_element_type=F32)
        bu_scr[2 * n] = res[:, :LANES]
        bu_scr[2 * n + 1] = res[:, LANES:]

    for c0 in range(0, S5_SLABS, S5_SCAN_SLABS):
        slabs = range(c0, c0 + S5_SCAN_SLABS)
        a_re = [jnp.broadcast_to(a_ref[0:1, c * LANES:(c + 1) * LANES], (SUBLANES, LANES)) for c in slabs]
        a_im = [jnp.broadcast_to(a_ref[1:2, c * LANES:(c + 1) * LANES], (SUBLANES, LANES)) for c in slabs]

        def step(t, carry, slabs=slabs, a_re=a_re, a_im=a_im):
            rsel = pl.ds(pl.multiple_of(t * SUBLANES, SUBLANES), SUBLANES)
            new = []
            for q, c in enumerate(slabs):
                h_re, h_im = carry[2 * q], carry[2 * q + 1]
                n_re = a_re[q] * h_re - a_im[q] * h_im + bu_scr[c, rsel, :]
                n_im = a_re[q] * h_im + a_im[q] * h_re + bu_scr[S5_SLABS + c, rsel, :]
                hs_scr[c, rsel, :] = n_re
                hs_scr[S5_SLABS + c, rsel, :] = n_im
                new += [n_re, n_im]
            return tuple(new)

        init = []
        for c in slabs:
            init += [h_scr[0, :, c * LANES:(c + 1) * LANES], h_scr[1, :, c * LANES:(c + 1) * LANES]]
        fin = lax.fori_loop(0, tl, step, tuple(init), unroll=min(S5_SCAN_UNROLL, tl))
        for q, c in enumerate(slabs):
            h_scr[0, :, c * LANES:(c + 1) * LANES] = fin[2 * q]
            h_scr[1, :, c * LANES:(c + 1) * LANES] = fin[2 * q + 1]

    slabs_per_out = S5_SLABS * MXU_TILE // S5_WIDTH
    y_tiles = []
    for m in range(S5_WIDTH // MXU_TILE):
        acc = None
        for part in range(2):
            for c in range(m * slabs_per_out, (m + 1) * slabs_per_out, 2):
                s = part * S5_SLABS + c
                hb = jnp.concatenate([hs_scr[s], hs_scr[s + 1]], axis=1).astype(BF16)
                term = jnp.dot(hb, cc_ref[s * LANES:(s + 2) * LANES, m * MXU_TILE:(m + 1) * MXU_TILE],
                               preferred_element_type=F32)
                acc = term if acc is None else acc + term
        y_tiles.append(acc)
    ch = jnp.concatenate(y_tiles, axis=1)
    if tl > 1:
        back = jnp.dot(perm_t_ref[...], _split_bf16x3(ch), preferred_element_type=F32)
        ch = back[:, :S5_WIDTH] + back[:, S5_WIDTH:2 * S5_WIDTH] + back[:, 2 * S5_WIDTH:]
    y = jax.nn.gelu(ch + d_ref[...] * u)
    y_ref[...] = y.reshape(y_ref.shape)
    hre_ref[...] = h_scr[0]
    him_ref[...] = h_scr[1]


def _s5_branch(u, h0_re, h0_im, bb, cc, a, d, tl):
    n_b, n_l, _ = u.shape
    rows = SUBLANES * tl
    if n_l == 1:
        u = u.reshape(n_b // SUBLANES, SUBLANES, S5_WIDTH)
        u_spec = pl.BlockSpec((None, SUBLANES, S5_WIDTH), lambda g, j: (g, 0, 0))
    else:
        u_spec = pl.BlockSpec((SUBLANES, tl, S5_WIDTH), lambda g, j: (g, j, 0))
    st_spec = pl.BlockSpec((SUBLANES, S5_FLAT), lambda g, j: (g, 0))

    def const(shape):
        return pl.BlockSpec(shape, lambda g, j: (0, 0))

    src = (jnp.arange(rows) % SUBLANES) * tl + jnp.arange(rows) // SUBLANES
    perm = (src[:, None] == jnp.arange(rows)[None, :]).astype(BF16)
    slab_scratch = pltpu.VMEM((2 * S5_SLABS, rows, LANES), F32)
    return pl.pallas_call(
        functools.partial(_s5_kernel, tl=tl),
        grid=(n_b // SUBLANES, n_l // tl),
        in_specs=[u_spec, st_spec, st_spec, const((S5_WIDTH, 2 * S5_FLAT)),
                  const((2 * S5_FLAT, S5_WIDTH)), const((2, S5_FLAT)), const((1, S5_WIDTH)),
                  const((rows, rows)), const((rows, rows))],
        out_specs=[u_spec, st_spec, st_spec],
        out_shape=[jax.ShapeDtypeStruct(u.shape, F32),
                   jax.ShapeDtypeStruct((n_b, S5_FLAT), F32),
                   jax.ShapeDtypeStruct((n_b, S5_FLAT), F32)],
        scratch_shapes=[slab_scratch, slab_scratch, pltpu.VMEM((2, SUBLANES, S5_FLAT), F32)],
        compiler_params=_cparams("parallel", "arbitrary"),
        name="s5_branch",
    )(u, h0_re, h0_im, bb, cc, a, d, perm, perm.T)


def _s5_params(lam_re, lam_im, log_dt, b_re, b_im, c_re, c_im, d_skip):
    dt = jnp.exp(log_dt)[:, None]
    mag = jnp.exp(lam_re * dt)
    ang = lam_im * dt
    ab_re, ab_im = mag * jnp.cos(ang), mag * jnp.sin(ang)
    den = jnp.square(lam_re) + jnp.square(lam_im)
    nr, ni = ab_re - 1.0, ab_im
    zf_re = (nr * lam_re + ni * lam_im) / den
    zf_im = (ni * lam_re - nr * lam_im) / den
    bb_re = zf_re[..., None] * b_re - zf_im[..., None] * b_im
    bb_im = zf_re[..., None] * b_im + zf_im[..., None] * b_re
    eye = jnp.eye(S5_GROUPS, dtype=F32)

    def in_blockdiag(m):
        return jnp.einsum('gph,gk->ghkp', m, eye).reshape(S5_WIDTH, S5_FLAT)

    def out_blockdiag(m):
        return jnp.einsum('ghp,gk->gpkh', m, eye).reshape(S5_FLAT, S5_WIDTH)

    bb = jnp.concatenate([in_blockdiag(bb_re), in_blockdiag(bb_im)], axis=1).astype(BF16)
    cc = jnp.concatenate([out_blockdiag(c_re), -out_blockdiag(c_im)], axis=0).astype(BF16)
    a = jnp.stack([ab_re.reshape(S5_FLAT), ab_im.reshape(S5_FLAT)])
    return bb, cc, a, d_skip.reshape(1, S5_WIDTH)


def _hgrn_kernel(hg_ref, s0_ref, lb_ref, gw_ref, y_ref, sout_ref, st_scr,
                 *, tl, valid_len, first_layer):
    j = pl.program_id(1)
    n_t = pl.num_programs(1)

    @pl.when(j == 0)
    def _():
        for h in range(HG_HEADS):
            st_scr[h] = s0_ref[0, 0, h].T

    row = lax.broadcasted_iota(jnp.int32, (HG_CHUNK, HG_WIDTH), 0)
    row_col = lax.broadcasted_iota(jnp.int32, (HG_CHUNK, 1), 0)
    lb = lb_ref[...]
    gw = gw_ref[...]

    def chunk_body(c, carry):
        r0 = pl.multiple_of(c * HG_CHUNK, HG_CHUNK)
        blk = hg_ref[0, pl.ds(r0, HG_CHUNK), :]
        q = blk[:, 0:HG_WIDTH]
        f_pre = blk[:, HG_WIDTH:2 * HG_WIDTH]
        v = blk[:, 2 * HG_WIDTH:3 * HG_WIDTH]
        g_out = blk[:, 3 * HG_WIDTH:4 * HG_WIDTH]
        if first_layer:
            log_f = jnp.minimum(f_pre, 0.0) - jnp.log1p(jnp.exp(-jnp.abs(f_pre)))
            k = jax.nn.sigmoid(-f_pre)
        else:
            log_f = jnp.log(lb + (1.0 - lb) * jax.nn.sigmoid(f_pre))
            k = (1.0 - lb) * jax.nn.sigmoid(-f_pre)
        if valid_len is not None:
            live = (j * tl + r0 + row) < valid_len
            log_f = jnp.where(live, log_f, 0.0)
            k = jnp.where(live, k, 0.0)
        b = log_f
        sh = 1
        while sh < HG_CHUNK:
            b = b + jnp.where(row >= sh, pltpu.roll(b, sh, 0), 0.0)
            sh *= 2

        n_rb = HG_CHUNK // SUBLANES
        for h in range(HG_HEADS):
            hs = slice(h * HG_HEAD_DIM, (h + 1) * HG_HEAD_DIM)
            qh, kh, vh, bh = q[:, hs], k[:, hs], v[:, hs], b[:, hs]
            b_last = bh[HG_CHUNK - 1:HG_CHUNK, :]
            st = st_scr[h]
            o = lax.dot_general((qh * jnp.exp(bh)).astype(BF16), st.astype(BF16),
                                (((1,), (1,)), ((), ())), preferred_element_type=F32)
            o_rb = []
            for i in range(n_rb):
                rs = slice(i * SUBLANES, (i + 1) * SUBLANES)
                q_i, b_i, acc = qh[rs], bh[rs], o[rs]
                for s in range((i + 1) * SUBLANES):
                    diff = b_i - bh[s:s + 1, :]
                    if s >= i * SUBLANES:
                        diff = jnp.minimum(diff, 0.0)
                    att = jnp.sum(q_i * kh[s:s + 1, :] * jnp.exp(diff), axis=-1, keepdims=True)
                    if s >= i * SUBLANES:
                        att = jnp.where(row_col[rs] >= s, att, 0.0)
                    acc = acc + att * vh[s:s + 1, :]
                o_rb.append(acc)
            o = jnp.concatenate(o_rb, axis=0)
            k_dec = kh * jnp.exp(b_last - bh)
            upd = lax.dot_general(vh.astype(BF16), k_dec.astype(BF16),
                                  (((0,), (0,)), ((), ())), preferred_element_type=F32)
            st_scr[h] = st * jnp.exp(b_last) + upd
            o = o * lax.rsqrt(jnp.mean(o * o, axis=-1, keepdims=True) + RMS_EPS) * gw
            gh = g_out[:, hs]
            y_ref[0, pl.ds(r0, HG_CHUNK), hs] = o * (gh * jax.nn.sigmoid(gh))
        return carry

    lax.fori_loop(0, tl // HG_CHUNK, chunk_body, 0, unroll=min(HG_UNROLL, tl // HG_CHUNK))

    @pl.when(j == n_t - 1)
    def _():
        for h in range(HG_HEADS):
            sout_ref[0, h] = st_scr[h].T


def _hgrn_branch(hg, s0_layers, layer, lb, gnorm_w, tl, valid_len, first_layer):
    n_b, n_l, _ = hg.shape
    st_shape = (HG_HEADS, HG_HEAD_DIM, HG_HEAD_DIM)
    return pl.pallas_call(
        functools.partial(_hgrn_kernel, tl=tl, valid_len=valid_len, first_layer=first_layer),
        grid=(n_b, n_l // tl),
        in_specs=[pl.BlockSpec((1, tl, 4 * HG_WIDTH), lambda b, j: (b, j, 0)),
                  pl.BlockSpec((1, 1) + st_shape, lambda b, j: (layer, b, 0, 0, 0)),
                  pl.BlockSpec((1, HG_WIDTH), lambda b, j: (0, 0)),
                  pl.BlockSpec((1, HG_HEAD_DIM), lambda b, j: (0, 0))],
        out_specs=[pl.BlockSpec((1, tl, HG_WIDTH), lambda b, j: (b, j, 0)),
                   pl.BlockSpec((1,) + st_shape, lambda b, j: (b, 0, 0, 0))],
        out_shape=[jax.ShapeDtypeStruct((n_b, n_l, HG_WIDTH), F32),
                   jax.ShapeDtypeStruct((n_b,) + st_shape, F32)],
        scratch_shapes=[pltpu.VMEM(st_shape, F32)],
        compiler_params=_cparams("parallel", "arbitrary"),
        name="hgrn_branch",
    )(hg, s0_layers, lb.reshape(1, HG_WIDTH), gnorm_w.reshape(1, HG_HEAD_DIM))


def _mix_kernel(ys5_ref, yhg_ref, gates_ref, x_ref, gate1_ref, shift2_ref, scale2_ref,
                wglu_ref, bglu_ref, wus_ref, wuh_ref, wout_ref, g_ref, b_ref, rw_ref, rb_ref,
                x1_ref, u2_ref, topi_ref, topw_ref):
    ys = ys5_ref[0]
    glu = ys * jax.nn.sigmoid(
        jnp.dot(ys.astype(BF16), wglu_ref[...], preferred_element_type=F32) + bglu_ref[...])
    gates = jax.nn.sigmoid(gates_ref[0])
    merged = (gates[:, :D_MODEL] * jnp.dot(glu.astype(BF16), wus_ref[...], preferred_element_type=F32)
              + gates[:, D_MODEL:] * jnp.dot(yhg_ref[0].astype(BF16), wuh_ref[...],
                                             preferred_element_type=F32))
    mix = jnp.dot(merged.astype(BF16), wout_ref[...], preferred_element_type=F32)
    x1 = _layer_norm(DEEPNORM_ALPHA * x_ref[0] + gate1_ref[0] * mix, g_ref[...], b_ref[...])
    x1_ref[0] = x1
    u2 = x1 * (1.0 + scale2_ref[0]) + shift2_ref[0]
    u2_ref[0] = u2
    u_hi = u2.astype(BF16)
    u_lo = (u2 - u_hi.astype(F32)).astype(BF16)
    hh_hl = jnp.dot(u_hi, rw_ref[...], preferred_element_type=F32)
    lh = jnp.dot(u_lo, rw_ref[:, :LANES], preferred_element_type=F32)
    logits = (hh_hl[:, :LANES] + hh_hl[:, LANES:]) + lh + rb_ref[...]
    lane = lax.broadcasted_iota(jnp.int32, logits.shape, 1)
    lane_f = lane.astype(F32)
    vals, idxs = [], []
    for _ in range(TOP_K):
        m = jnp.max(logits, axis=-1, keepdims=True)
        idx = jnp.min(jnp.where(logits == m, lane_f, float(LANES)), axis=-1,
                      keepdims=True).astype(jnp.int32)
        vals.append(m)
        idxs.append(idx)
        logits = jnp.where(lane == idx, NEG_BIG, logits)
    exps = [jnp.exp(val - vals[0]) for val in vals]
    denom = exps[0] + exps[1] + exps[2] + exps[3]
    topi = jnp.full(lane.shape, -1, jnp.int32)
    topw = jnp.zeros(lane.shape, F32)
    for kk in range(TOP_K):
        topi = jnp.where(lane == kk, idxs[kk], topi)
        topw = jnp.where(lane == kk, exps[kk] / denom, topw)
    topi_ref[0] = topi
    topw_ref[0] = topw


def _mix(ys5, yhg, gates, x, gate1, shift2, scale2, lw, tm, per_row):
    n_g, n_r, _ = x.shape

    def row_spec(width):
        return pl.BlockSpec((1, tm, width), lambda g, i: (g, i, 0))

    def const(shape):
        return pl.BlockSpec(shape, lambda g, i: (0, 0))

    ms = _mod_spec(tm, per_row)
    return pl.pallas_call(
        _mix_kernel,
        grid=(n_g, n_r // tm),
        in_specs=[row_spec(S5_WIDTH), row_spec(HG_WIDTH), row_spec(2 * D_MODEL), row_spec(D_MODEL),
                  ms, ms, ms,
                  const((S5_WIDTH, S5_WIDTH)), const((1, S5_WIDTH)),
                  const((S5_WIDTH, D_MODEL)), const((HG_WIDTH, D_MODEL)),
                  const((D_MODEL, D_MODEL)), const((1, D_MODEL)), const((1, D_MODEL)),
                  const((D_MODEL, 2 * LANES)), const((1, LANES))],
        out_specs=[row_spec(D_MODEL), row_spec(D_MODEL), row_spec(LANES), row_spec(LANES)],
        out_shape=[jax.ShapeDtypeStruct((n_g, n_r, D_MODEL), F32),
                   jax.ShapeDtypeStruct((n_g, n_r, D_MODEL), F32),
                   jax.ShapeDtypeStruct((n_g, n_r, LANES), jnp.int32),
                   jax.ShapeDtypeStruct((n_g, n_r, LANES), F32)],
        compiler_params=_cparams("parallel", "parallel"),
        name="branch_mix",
    )(ys5, yhg, gates, x, gate1, shift2, scale2,
      lw['w_glu'], lw['b_glu'], lw['w_up_s5'], lw['w_up_hg'], lw['w_out'],
      lw['ln1_g'], lw['ln1_b'], lw['router_w'], lw['router_b'])


RANK_TILE = 384


def _expert_onehot(top_idx, kk):
    lane = lax.broadcasted_iota(jnp.int32, top_idx.shape, 1)
    return (top_idx[:, kk:kk + 1] == lane).astype(F32)


def _count_kernel(topi_ref, cnt_ref):
    @pl.when(pl.program_id(0) == 0)
    def _():
        cnt_ref[...] = jnp.zeros(cnt_ref.shape, F32)

    top_idx = topi_ref[...]
    acc = cnt_ref[...]
    for kk in range(TOP_K):
        acc = acc + jnp.sum(_expert_onehot(top_idx, kk), axis=0, keepdims=True)
    cnt_ref[...] = acc


def _rank_kernel(topi_ref, start_ref, dest_ref, off_scr):
    @pl.when(pl.program_id(0) == 0)
    def _():
        off_scr[...] = start_ref[...]

    top_idx = topi_ref[...]
    tm = top_idx.shape[0]
    lane = lax.broadcasted_iota(jnp.int32, top_idx.shape, 1)
    earlier = (lax.broadcasted_iota(jnp.int32, (tm, tm), 1)
               < lax.broadcasted_iota(jnp.int32, (tm, tm), 0)).astype(BF16)
    off = off_scr[...]
    dest = jnp.zeros(top_idx.shape, jnp.int32)
    for kk in range(TOP_K):
        onehot = _expert_onehot(top_idx, kk)
        before = jnp.dot(earlier, onehot.astype(BF16), preferred_element_type=F32)
        slot = jnp.sum(onehot * (before + off), axis=-1, keepdims=True)
        dest = jnp.where(lane == kk, slot.astype(jnp.int32), dest)
        off = off + jnp.sum(onehot, axis=0, keepdims=True)
    dest_ref[...] = dest
    off_scr[...] = off


def _slot_tables(top_idx_all):
    n_tok = top_idx_all.shape[0]
    n_tiles = n_tok // RANK_TILE
    tile_spec = pl.BlockSpec((RANK_TILE, LANES), lambda i: (i, 0))
    lane_spec = pl.BlockSpec((1, LANES), lambda i: (0, 0))
    counts = pl.pallas_call(
        _count_kernel, grid=(n_tiles,), in_specs=[tile_spec], out_specs=lane_spec,
        out_shape=jax.ShapeDtypeStruct((1, LANES), F32),
        compiler_params=_cparams("arbitrary"), name="expert_count",
    )(top_idx_all)
    counts = counts[0, :N_EXPERTS].astype(jnp.int32)
    padded = (counts + MOE_BLOCK - 1) // MOE_BLOCK * MOE_BLOCK
    pad_end = jnp.cumsum(padded)
    pad_start = pad_end - padded
    start_row = jnp.zeros((1, LANES), F32).at[0, :N_EXPERTS].set(pad_start.astype(F32))
    dest = pl.pallas_call(
        _rank_kernel, grid=(n_tiles,), in_specs=[tile_spec, lane_spec], out_specs=tile_spec,
        out_shape=jax.ShapeDtypeStruct((n_tok, LANES), jnp.int32),
        scratch_shapes=[pltpu.VMEM((1, LANES), F32)],
        compiler_params=_cparams("arbitrary"), name="expert_rank",
    )(top_idx_all, start_row)
    n_blocks = -(-(n_tok * TOP_K) // MOE_BLOCK) + N_EXPERTS
    block_lo = jnp.arange(n_blocks, dtype=jnp.int32) * MOE_BLOCK
    block_e = jnp.minimum(jnp.sum((pad_end[None, :] <= block_lo[:, None]).astype(jnp.int32), axis=1),
                          N_EXPERTS - 1).astype(jnp.int32)
    n_used = (pad_end[-1] // MOE_BLOCK).astype(jnp.int32).reshape(1)
    return dest[:, :TOP_K].reshape(-1), pad_end.astype(jnp.int32), block_e, n_used, n_blocks


def _dispatch_kernel(dest_ref, pad_end_ref, n_used_ref, up_ref, us_ref, xb_hbm, zero_scr, sem,
                     *, tm_p, n_tiles_p, n_blocks):
    i = pl.program_id(0)

    def zero_block(row0):
        return pltpu.make_async_copy(zero_scr, xb_hbm.at[pl.ds(row0, MOE_BLOCK)], sem)

    @pl.when(i == 0)
    def _():
        zero_scr[...] = jnp.zeros(zero_scr.shape, F32)

        def pad_block(e):
            start = pad_end_ref[e - 1] if e else 0
            end = pad_end_ref[e]
            return end > start, pl.multiple_of(jnp.maximum(end - MOE_BLOCK, 0), MOE_BLOCK)

        def tail_start(blk, carry):
            zero_block(pl.multiple_of(blk * MOE_BLOCK, MOE_BLOCK)).start()
            return carry

        def tail_wait(blk, carry):
            zero_block(pl.multiple_of(blk * MOE_BLOCK, MOE_BLOCK)).wait()
            return carry

        for e in range(N_EXPERTS):
            live, row0 = pad_block(e)
            pl.when(live)(lambda row0=row0: zero_block(row0).start())
        lax.fori_loop(n_used_ref[0], n_blocks, tail_start, 0)
        for e in range(N_EXPERTS):
            live, row0 = pad_block(e)
            pl.when(live)(lambda row0=row0: zero_block(row0).wait())
        lax.fori_loop(n_used_ref[0], n_blocks, tail_wait, 0)

    def scatter(src_ref, n_rows, first_tok):
        def issue(r, carry):
            for kk in range(TOP_K):
                slot = dest_ref[(first_tok + r) * TOP_K + kk]
                pltpu.make_async_copy(src_ref.at[pl.ds(r, 1)], xb_hbm.at[pl.ds(slot, 1)], sem).start()
            return carry

        lax.fori_loop(0, n_rows, issue, 0, unroll=DMA_ISSUE_UNROLL)
        for kk in range(TOP_K):
            pltpu.make_async_copy(src_ref, xb_hbm.at[pl.ds(0, n_rows)], sem).wait()

    @pl.when(i < n_tiles_p)
    def _():
        scatter(up_ref, tm_p, i * tm_p)

    @pl.when(i == n_tiles_p)
    def _():
        scatter(us_ref, us_ref.shape[0], n_tiles_p * tm_p)


def _dispatch(dest, pad_end, n_used, u_prompt, u_sample, n_blocks, tm_p):
    n_tiles_p = u_prompt.shape[0] // tm_p
    n_s = u_sample.shape[0]
    grid_spec = pltpu.PrefetchScalarGridSpec(
        num_scalar_prefetch=3, grid=(n_tiles_p + 1,),
        in_specs=[pl.BlockSpec((tm_p, D_MODEL), lambda i, d, p, nu: (jnp.minimum(i, n_tiles_p - 1), 0)),
                  pl.BlockSpec((n_s, D_MODEL), lambda i, d, p, nu: (0, 0))],
        out_specs=pl.BlockSpec(memory_space=pl.ANY),
        scratch_shapes=[pltpu.VMEM((MOE_BLOCK, D_MODEL), F32), pltpu.SemaphoreType.DMA(())])
    return pl.pallas_call(
        functools.partial(_dispatch_kernel, tm_p=tm_p, n_tiles_p=n_tiles_p, n_blocks=n_blocks),
        grid_spec=grid_spec,
        out_shape=jax.ShapeDtypeStruct((n_blocks * MOE_BLOCK, D_MODEL), F32),
        compiler_params=_cparams("arbitrary"),
        name="moe_dispatch",
    )(dest, pad_end, n_used, u_prompt, u_sample)


def _moe_kernel(block_e_ref, n_used_ref, x_ref, wgu_ref, bgu_ref, wd_ref, bd_ref, y_ref,
                wgu_bf, wd_bf):
    i = pl.program_id(0)
    n_used = n_used_ref[0]
    e = block_e_ref[i]
    e_prev = block_e_ref[jnp.maximum(i - 1, 0)]

    @pl.when(jnp.logical_and(i < n_used, jnp.logical_or(i == 0, e != e_prev)))
    def _():
        wgu_bf[...] = wgu_ref[0, 0].astype(BF16)
        wd_bf[...] = wd_ref[0, 0].astype(BF16)

    @pl.when(i < n_used)
    def _():
        h = jnp.dot(x_ref[...].astype(BF16), wgu_bf[...], preferred_element_type=F32) + bgu_ref[0, 0]
        gate = jnp.minimum(h[:, :D_EXPERT], SWIGLU_LIMIT)
        up = jnp.clip(h[:, D_EXPERT:], -SWIGLU_LIMIT, SWIGLU_LIMIT)
        act = gate * jax.nn.sigmoid(SWIGLU_ALPHA * gate) * (up + 1.0)
        y_ref[...] = jnp.dot(act.astype(BF16), wd_bf[...], preferred_element_type=F32) + bd_ref[0, 0]

    @pl.when(i >= n_used)
    def _():
        y_ref[...] = jnp.zeros(y_ref.shape, F32)


def _moe_experts(xb, block_e, n_used, layer, w_gate_up, b_gate_up, w_down, b_down):
    n_blocks = block_e.shape[0]

    def expert_spec(shape):
        return pl.BlockSpec((1, 1) + shape, lambda i, be, nu: (layer, be[i], 0, 0))

    grid_spec = pltpu.PrefetchScalarGridSpec(
        num_scalar_prefetch=2,
        grid=(n_blocks,),
        in_specs=[pl.BlockSpec((MOE_BLOCK, D_MODEL), lambda i, be, nu: (jnp.minimum(i, nu[0] - 1), 0)),
                  expert_spec((D_MODEL, 2 * D_EXPERT)), expert_spec((1, 2 * D_EXPERT)),
                  expert_spec((D_EXPERT, D_MODEL)), expert_spec((1, D_MODEL))],
        out_specs=pl.BlockSpec((MOE_BLOCK, D_MODEL), lambda i, be, nu: (i, 0)),
        scratch_shapes=[pltpu.VMEM((D_MODEL, 2 * D_EXPERT), BF16),
                        pltpu.VMEM((D_EXPERT, D_MODEL), BF16)],
    )
    return pl.pallas_call(
        _moe_kernel,
        grid_spec=grid_spec,
        out_shape=jax.ShapeDtypeStruct((n_blocks * MOE_BLOCK, D_MODEL), F32),
        compiler_params=_cparams("arbitrary"),
        name="moe_experts",
    )(block_e, n_used, xb, w_gate_up,
      b_gate_up.reshape(DEPTH, N_EXPERTS, 1, 2 * D_EXPERT),
      w_down, b_down.reshape(DEPTH, N_EXPERTS, 1, D_MODEL))


def _combine_kernel(dest_ref, yb_hbm, x1_ref, topw_ref, gate2_ref, g_ref, b_ref, o_ref, ybuf, sem,
                    *, tm, tok_offset):
    g = pl.program_id(0)
    i = pl.program_id(1)
    base = (tok_offset + (g * pl.num_programs(1) + i) * tm) * TOP_K

    def issue(r, carry):
        for kk in range(TOP_K):
            slot = dest_ref[base + r * TOP_K + kk]
            pltpu.make_async_copy(yb_hbm.at[pl.ds(slot, 1)], ybuf.at[kk, pl.ds(r, 1)], sem).start()
        return carry

    lax.fori_loop(0, tm, issue, 0, unroll=DMA_ISSUE_UNROLL)
    for kk in range(TOP_K):
        pltpu.make_async_copy(yb_hbm.at[pl.ds(0, tm)], ybuf.at[kk], sem).wait()
    topw = topw_ref[0]
    ffn = ((topw[:, 0:1] * ybuf[0] + topw[:, 1:2] * ybuf[1])
           + (topw[:, 2:3] * ybuf[2] + topw[:, 3:4] * ybuf[3]))
    o_ref[0] = _layer_norm(DEEPNORM_ALPHA * x1_ref[0] + gate2_ref[0] * ffn, g_ref[...], b_ref[...])


def _combine(dest, yb, x1, topw, gate2, ln_g, ln_b, tm, per_row, tok_offset):
    n_g, n_r, _ = x1.shape
    ms = (pl.BlockSpec((1, tm, D_MODEL), lambda g, i, d: (g, i, 0)) if per_row
          else pl.BlockSpec((1, 1, D_MODEL), lambda g, i, d: (g, 0, 0)))
    grid_spec = pltpu.PrefetchScalarGridSpec(
        num_scalar_prefetch=1,
        grid=(n_g, n_r // tm),
        in_specs=[pl.BlockSpec(memory_space=pl.ANY),
                  pl.BlockSpec((1, tm, D_MODEL), lambda g, i, d: (g, i, 0)),
                  pl.BlockSpec((1, tm, LANES), lambda g, i, d: (g, i, 0)),
                  ms,
                  pl.BlockSpec((1, D_MODEL), lambda g, i, d: (0, 0)),
                  pl.BlockSpec((1, D_MODEL), lambda g, i, d: (0, 0))],
        out_specs=pl.BlockSpec((1, tm, D_MODEL), lambda g, i, d: (g, i, 0)),
        scratch_shapes=[pltpu.VMEM((TOP_K, tm, D_MODEL), F32), pltpu.SemaphoreType.DMA(())],
    )
    return pl.pallas_call(
        functools.partial(_combine_kernel, tm=tm, tok_offset=tok_offset),
        grid_spec=grid_spec,
        out_shape=jax.ShapeDtypeStruct(x1.shape, F32),
        compiler_params=_cparams("arbitrary", "arbitrary"),
        name="moe_combine",
    )(dest, yb, x1, topw, gate2, ln_g, ln_b)


def kernel(x_prompt, x_sample, state_s5_re, state_s5_im, state_hgrn, c_prompt, c_sample,
           w_ada, b_ada, w_in, s5_lambda_re, s5_lambda_im, s5_log_dt, s5_b_re, s5_b_im,
           s5_c_re, s5_c_im, s5_d, w_glu, b_glu, w_up_s5, hgrn_lb_logits, hgrn_gnorm_w,
           w_up_hg, w_out, ln1_g, ln1_b, router_w, router_b, w_gate_up, b_gate_up,
           w_down, b_down, ln2_g, ln2_b):
    n_p, seq, _ = x_prompt.shape
    n_s = x_sample.shape[0]
    n_tok_p = n_p * seq

    lb_p = jax.nn.softmax(hgrn_lb_logits.astype(F32), axis=0)
    lb_all = jnp.cumsum(lb_p, axis=0) - lb_p[0]

    mod = _modulation(jnp.concatenate([c_prompt, c_sample], axis=0), w_ada, b_ada)

    def mods(layer, which):
        sl = slice(which * D_MODEL, (which + 1) * D_MODEL)
        return (mod[layer, :n_p, sl].reshape(n_p, 1, D_MODEL),
                mod[layer, n_p:, sl].reshape(1, n_s, D_MODEL))

    xp = x_prompt
    xs = x_sample.reshape(1, n_s, D_MODEL)
    zero_s5 = jnp.zeros((n_p, S5_FLAT), F32)
    zero_hg = jnp.zeros((1, n_p, HG_HEADS, HG_HEAD_DIM, HG_HEAD_DIM), F32)
    outs = {k: [] for k in ('p_re', 'p_im', 'p_hg', 's_re', 's_im', 's_hg')}

    for l in range(DEPTH):
        shift1, scale1, gate1, shift2, scale2, gate2 = [mods(l, w) for w in range(6)]
        w_in_bf = w_in[l].astype(BF16)
        bb, cc, a, d = _s5_params(s5_lambda_re[l], s5_lambda_im[l], s5_log_dt[l], s5_b_re[l],
                                  s5_b_im[l], s5_c_re[l], s5_c_im[l], s5_d[l])
        router_w_pad = jnp.zeros((D_MODEL, LANES), F32).at[:, :N_EXPERTS].set(router_w[l])
        router_w_hi = router_w_pad.astype(BF16)
        router_w_lo = (router_w_pad - router_w_hi.astype(F32)).astype(BF16)
        router_b_pad = jnp.full((1, LANES), NEG_BIG, F32).at[0, :N_EXPERTS].set(router_b[l])
        lw = {'w_glu': w_glu[l].astype(BF16), 'b_glu': b_glu[l].reshape(1, S5_WIDTH),
              'w_up_s5': w_up_s5[l].astype(BF16), 'w_up_hg': w_up_hg[l].astype(BF16),
              'w_out': w_out[l].astype(BF16), 'ln1_g': ln1_g[l].reshape(1, D_MODEL),
              'ln1_b': ln1_b[l].reshape(1, D_MODEL), 'router_w': jnp.concatenate([router_w_hi, router_w_lo], axis=1),
              'router_b': router_b_pad}
        first = l == 0

        us5_p, hg_p, gates_p = _input_proj(xp, shift1[0], scale1[0], w_in_bf, 256, False)
        ys5_p, pre, pim = _s5_branch(us5_p, zero_s5, zero_s5, bb, cc, a, d, 64)
        yhg_p, phg = _hgrn_branch(hg_p, zero_hg, 0, lb_all[l], hgrn_gnorm_w[l], 128, None, first)
        x1_p, u2_p, ti_p, tw_p = _mix(ys5_p, yhg_p, gates_p, xp, gate1[0], shift2[0], scale2[0],
                                      lw, 256, False)
        us5_s, hg_s, gates_s = _input_proj(xs, shift1[1], scale1[1], w_in_bf, n_s, True)
        ys5_s, sre, sim = _s5_branch(us5_s.reshape(n_s, 1, S5_WIDTH),
                                     state_s5_re[l].reshape(n_s, S5_FLAT),
                                     state_s5_im[l].reshape(n_s, S5_FLAT), bb, cc, a, d, 1)
        hg_s_pad = jnp.pad(hg_s.reshape(n_s, 1, 4 * HG_WIDTH), ((0, 0), (0, HG_CHUNK - 1), (0, 0)))
        yhg_s, shg = _hgrn_branch(hg_s_pad, state_hgrn, l, lb_all[l], hgrn_gnorm_w[l],
                                  HG_CHUNK, 1, first)
        x1_s, u2_s, ti_s, tw_s = _mix(ys5_s.reshape(1, n_s, S5_WIDTH),
                                      yhg_s[:, 0, :].reshape(1, n_s, HG_WIDTH), gates_s, xs,
                                      gate1[1], shift2[1], scale2[1], lw, n_s, True)

        top_idx_all = jnp.concatenate([ti_p.reshape(n_tok_p, LANES), ti_s.reshape(n_s, LANES)], axis=0)
        dest, pad_end, block_e, n_used, n_blocks = _slot_tables(top_idx_all)
        xb = _dispatch(dest, pad_end, n_used, u2_p.reshape(n_tok_p, D_MODEL),
                       u2_s.reshape(n_s, D_MODEL), n_blocks, 512)
        yb = _moe_experts(xb, block_e, n_used, l, w_gate_up, b_gate_up, w_down, b_down)
        g2, b2 = ln2_g[l].reshape(1, D_MODEL), ln2_b[l].reshape(1, D_MODEL)
        xp = _combine(dest, yb, x1_p, tw_p, gate2[0], g2, b2, 256, False, 0)
        xs = _combine(dest, yb, x1_s, tw_s, gate2[1], g2, b2, n_s, True, n_tok_p)

        outs['p_re'].append(pre.reshape(n_p, S5_GROUPS, S5_STATE))
        outs['p_im'].append(pim.reshape(n_p, S5_GROUPS, S5_STATE))
        outs['p_hg'].append(phg)
        outs['s_re'].append(sre.reshape(n_s, S5_GROUPS, S5_STATE))
        outs['s_im'].append(sim.reshape(n_s, S5_GROUPS, S5_STATE))
        outs['s_hg'].append(shg)

    return (xp, xs.reshape(n_s, 1, D_MODEL),
            jnp.stack(outs['p_re']), jnp.stack(outs['p_im']), jnp.stack(outs['p_hg']),
            jnp.stack(outs['s_re']), jnp.stack(outs['s_im']), jnp.stack(outs['s_hg']))
```

```python
import functools

import jax
import jax.numpy as jnp
from jax import lax
from jax.experimental import pallas as pl
from jax.experimental.pallas import tpu as pltpu

F32 = jnp.float32
BF16 = jnp.bfloat16

LANES = 128
SUBLANES = 8
MXU_TILE = 256
VMEM_LIMIT = 56 * 1024 * 1024
NEG_BIG = -1e30

D_MODEL = 1024
DEPTH = 2
S5_WIDTH = 512
S5_GROUP = 16
S5_GROUPS = 32
S5_STATE = 64
S5_FLAT = S5_GROUPS * S5_STATE
HG_WIDTH = 512
HG_HEAD_DIM = 128
HG_HEADS = 4
HG_CHUNK = 16
HG_UNROLL = 2
RMS_EPS = 1e-6
N_IN = S5_WIDTH + 4 * HG_WIDTH + 2 * D_MODEL
N_EXPERTS = 32
TOP_K = 4
D_EXPERT = D_MODEL
SWIGLU_LIMIT = 7.0
SWIGLU_ALPHA = 1.702
MOE_BLOCK = 512
DMA_ISSUE_UNROLL = 4
DEEPNORM_ALPHA = (2 * DEPTH) ** 0.25
LN_EPS = 1e-5


def _cparams(*sem):
    return pltpu.CompilerParams(dimension_semantics=sem, vmem_limit_bytes=VMEM_LIMIT)


def _layer_norm(x, g, b):
    mu = jnp.mean(x, axis=-1, keepdims=True)
    xc = x - mu
    var = jnp.mean(xc * xc, axis=-1, keepdims=True)
    return xc * lax.rsqrt(var + LN_EPS) * g + b


def _mod_kernel(c_ref, w_ref, b_ref, o_ref):
    c = c_ref[...]
    s = c * jax.nn.sigmoid(c)
    o_ref[0] = jnp.dot(s.astype(BF16), w_ref[0].astype(BF16),
                       preferred_element_type=F32) + b_ref[0]


def _modulation(c_all, w_ada, b_ada):
    n_rows = c_all.shape[0]
    tn = 1536
    return pl.pallas_call(
        _mod_kernel,
        grid=(DEPTH, 6 * D_MODEL // tn),
        in_specs=[
            pl.BlockSpec((n_rows, D_MODEL), lambda l, j: (0, 0)),
            pl.BlockSpec((1, D_MODEL, tn), lambda l, j: (l, 0, j)),
            pl.BlockSpec((1, 1, tn), lambda l, j: (l, 0, j)),
        ],
        out_specs=pl.BlockSpec((1, n_rows, tn), lambda l, j: (l, 0, j)),
        out_shape=jax.ShapeDtypeStruct((DEPTH, n_rows, 6 * D_MODEL), F32),
        compiler_params=_cparams("parallel", "parallel"),
        name="adaln_mod",
    )(c_all, w_ada, b_ada.reshape(DEPTH, 1, 6 * D_MODEL))


def _proj_kernel(x_ref, shift_ref, scale_ref, w_ref, us5_ref, hg_ref, gates_ref):
    u = x_ref[0] * (1.0 + scale_ref[0]) + shift_ref[0]
    ub = u.astype(BF16)
    c0, c1 = S5_WIDTH, S5_WIDTH + 4 * HG_WIDTH
    us5_ref[0] = jnp.dot(ub, w_ref[:, :c0], preferred_element_type=F32)
    hg_ref[0] = jnp.dot(ub, w_ref[:, c0:c1], preferred_element_type=F32)
    gates_ref[0] = jnp.dot(ub, w_ref[:, c1:], preferred_element_type=F32)


def _mod_spec(tm, per_row):
    if per_row:
        return pl.BlockSpec((1, tm, D_MODEL), lambda g, i: (g, i, 0))
    return pl.BlockSpec((1, 1, D_MODEL), lambda g, i: (g, 0, 0))


def _input_proj(x, shift, scale, w_in_bf, tm, per_row):
    n_g, n_r, _ = x.shape

    def row_spec(width):
        return pl.BlockSpec((1, tm, width), lambda g, i: (g, i, 0))

    return pl.pallas_call(
        _proj_kernel,
        grid=(n_g, n_r // tm),
        in_specs=[row_spec(D_MODEL), _mod_spec(tm, per_row), _mod_spec(tm, per_row),
                  pl.BlockSpec((D_MODEL, N_IN), lambda g, i: (0, 0))],
        out_specs=[row_spec(S5_WIDTH), row_spec(4 * HG_WIDTH), row_spec(2 * D_MODEL)],
        out_shape=[jax.ShapeDtypeStruct((n_g, n_r, S5_WIDTH), F32),
                   jax.ShapeDtypeStruct((n_g, n_r, 4 * HG_WIDTH), F32),
                   jax.ShapeDtypeStruct((n_g, n_r, 2 * D_MODEL), F32)],
        compiler_params=_cparams("parallel", "parallel"),
        name="input_proj",
    )(x, shift, scale, w_in_bf)


S5_SLABS = S5_FLAT // LANES
S5_SCAN_SLABS = 8
S5_SCAN_UNROLL = 4


def _split_bf16x3(x):
    p1 = x.astype(BF16)
    r1 = x - p1.astype(F32)
    p2 = r1.astype(BF16)
    p3 = (r1 - p2.astype(F32)).astype(BF16)
    return jnp.concatenate([p1, p2, p3], axis=1)


def _s5_kernel(u_ref, h0re_ref, h0im_ref, bb_ref, cc_ref, a_ref, d_ref, perm_ref, perm_t_ref,
               y_ref, hre_ref, him_ref, bu_scr, hs_scr, h_scr, *, tl):
    j = pl.program_id(1)
    rows = SUBLANES * tl
    u = u_ref[...].reshape(rows, S5_WIDTH)
    ub = u.astype(BF16)
    if tl > 1:
        ub = jnp.dot(perm_ref[...], ub, preferred_element_type=F32).astype(BF16)

    @pl.when(j == 0)
    def _():
        h_scr[0] = h0re_ref[...]
        h_scr[1] = h0im_ref[...]

    cols_per_in_tile = MXU_TILE * (S5_STATE // S5_GROUP)
    for n in range(2 * S5_FLAT // MXU_TILE):
        kt = (n * MXU_TILE % S5_FLAT) // cols_per_in_tile
        res = jnp.dot(ub[:, kt * MXU_TILE:(kt + 1) * MXU_TILE],
                      bb_ref[kt * MXU_TILE:(kt + 1) * MXU_TILE, n * MXU_TILE:(n + 1) * MXU_TILE],
                      preferred_element_type=F32)
        bu_scr[2 * n] = res[:, :LANES]
        bu_scr[2 * n + 1] = res[:, LANES:]

    for c0 in range(0, S5_SLABS, S5_SCAN_SLABS):
        slabs = range(c0, c0 + S5_SCAN_SLABS)
        a_re = [jnp.broadcast_to(a_ref[0:1, c * LANES:(c + 1) * LANES], (SUBLANES, LANES)) for c in slabs]
        a_im = [jnp.broadcast_to(a_ref[1:2, c * LANES:(c + 1) * LANES], (SUBLANES, LANES)) for c in slabs]

        def step(t, carry, slabs=slabs, a_re=a_re, a_im=a_im):
            rsel = pl.ds(pl.multiple_of(t * SUBLANES, SUBLANES), SUBLANES)
            new = []
            for q, c in enumerate(slabs):
                h_re, h_im = carry[2 * q], carry[2 * q + 1]
                n_re = a_re[q] * h_re - a_im[q] * h_im + bu_scr[c, rsel, :]
                n_im = a_re[q] * h_im + a_im[q] * h_re + bu_scr[S5_SLABS + c, rsel, :]
                hs_scr[c, rsel, :] = n_re
                hs_scr[S5_SLABS + c, rsel, :] = n_im
                new += [n_re, n_im]
            return tuple(new)

        init = []
        for c in slabs:
            init += [h_scr[0, :, c * LANES:(c + 1) * LANES], h_scr[1, :, c * LANES:(c + 1) * LANES]]
        fin = lax.fori_loop(0, tl, step, tuple(init), unroll=min(S5_SCAN_UNROLL, tl))
        for q, c in enumerate(slabs):
            h_scr[0, :, c * LANES:(c + 1) * LANES] = fin[2 * q]
            h_scr[1, :, c * LANES:(c + 1) * LANES] = fin[2 * q + 1]

    slabs_per_out = S5_SLABS * MXU_TILE // S5_WIDTH
    y_tiles = []
    for m in range(S5_WIDTH // MXU_TILE):
        acc = None
        for part in range(2):
            for c in range(m * slabs_per_out, (m + 1) * slabs_per_out, 2):
                s = part * S5_SLABS + c
                hb = jnp.concatenate([hs_scr[s], hs_scr[s + 1]], axis=1).astype(BF16)
                term = jnp.dot(hb, cc_ref[s * LANES:(s + 2) * LANES, m * MXU_TILE:(m + 1) * MXU_TILE],
                               preferred_element_type=F32)
                acc = term if acc is None else acc + term
        y_tiles.append(acc)
    ch = jnp.concatenate(y_tiles, axis=1)
    if tl > 1:
        back = jnp.dot(perm_t_ref[...], _split_bf16x3(ch), preferred_element_type=F32)
        ch = back[:, :S5_WIDTH] + back[:, S5_WIDTH:2 * S5_WIDTH] + back[:, 2 * S5_WIDTH:]
    y = jax.nn.gelu(ch + d_ref[...] * u)
    y_ref[...] = y.reshape(y_ref.shape)
    hre_ref[...] = h_scr[0]
    him_ref[...] = h_scr[1]


def _s5_branch(u, h0_re, h0_im, bb, cc, a, d, tl):
    n_b, n_l, _ = u.shape
    rows = SUBLANES * tl
    if n_l == 1:
        u = u.reshape(n_b // SUBLANES, SUBLANES, S5_WIDTH)
        u_spec = pl.BlockSpec((None, SUBLANES, S5_WIDTH), lambda g, j: (g, 0, 0))
    else:
        u_spec = pl.BlockSpec((SUBLANES, tl, S5_WIDTH), lambda g, j: (g, j, 0))
    st_spec = pl.BlockSpec((SUBLANES, S5_FLAT), lambda g, j: (g, 0))

    def const(shape):
        return pl.BlockSpec(shape, lambda g, j: (0, 0))

    src = (jnp.arange(rows) % SUBLANES) * tl + jnp.arange(rows) // SUBLANES
    perm = (src[:, None] == jnp.arange(rows)[None, :]).astype(BF16)
    slab_scratch = pltpu.VMEM((2 * S5_SLABS, rows, LANES), F32)
    return pl.pallas_call(
        functools.partial(_s5_kernel, tl=tl),
        grid=(n_b // SUBLANES, n_l // tl),
        in_specs=[u_spec, st_spec, st_spec, const((S5_WIDTH, 2 * S5_FLAT)),
                  const((2 * S5_FLAT, S5_WIDTH)), const((2, S5_FLAT)), const((1, S5_WIDTH)),
                  const((rows, rows)), const((rows, rows))],
        out_specs=[u_spec, st_spec, st_spec],
        out_shape=[jax.ShapeDtypeStruct(u.shape, F32),
                   jax.ShapeDtypeStruct((n_b, S5_FLAT), F32),
                   jax.ShapeDtypeStruct((n_b, S5_FLAT), F32)],
        scratch_shapes=[slab_scratch, slab_scratch, pltpu.VMEM((2, SUBLANES, S5_FLAT), F32)],
        compiler_params=_cparams("parallel", "arbitrary"),
        name="s5_branch",
    )(u, h0_re, h0_im, bb, cc, a, d, perm, perm.T)


def _s5_params(lam_re, lam_im, log_dt, b_re, b_im, c_re, c_im, d_skip):
    dt = jnp.exp(log_dt)[:, None]
    mag = jnp.exp(lam_re * dt)
    ang = lam_im * dt
    ab_re, ab_im = mag * jnp.cos(ang), mag * jnp.sin(ang)
    den = jnp.square(lam_re) + jnp.square(lam_im)
    nr, ni = ab_re - 1.0, ab_im
    zf_re = (nr * lam_re + ni * lam_im) / den
    zf_im = (ni * lam_re - nr * lam_im) / den
    bb_re = zf_re[..., None] * b_re - zf_im[..., None] * b_im
    bb_im = zf_re[..., None] * b_im + zf_im[..., None] * b_re
    eye = jnp.eye(S5_GROUPS, dtype=F32)

    def in_blockdiag(m):
        return jnp.einsum('gph,gk->ghkp', m, eye).reshape(S5_WIDTH, S5_FLAT)

    def out_blockdiag(m):
        return jnp.einsum('ghp,gk->gpkh', m, eye).reshape(S5_FLAT, S5_WIDTH)

    bb = jnp.concatenate([in_blockdiag(bb_re), in_blockdiag(bb_im)], axis=1).astype(BF16)
    cc = jnp.concatenate([out_blockdiag(c_re), -out_blockdiag(c_im)], axis=0).astype(BF16)
    a = jnp.stack([ab_re.reshape(S5_FLAT), ab_im.reshape(S5_FLAT)])
    return bb, cc, a, d_skip.reshape(1, S5_WIDTH)


def _hgrn_kernel(hg_ref, s0_ref, lb_ref, gw_ref, y_ref, sout_ref, st_scr,
                 *, tl, valid_len, first_layer, single_chunk):
    j = pl.program_id(1)
    n_t = pl.num_programs(1)

    @pl.when(j == 0)
    def _():
        for h in range(HG_HEADS):
            st_scr[h] = s0_ref[0, 0, h].T

    row = lax.broadcasted_iota(jnp.int32, (HG_CHUNK, HG_WIDTH), 0)
    row_col = lax.broadcasted_iota(jnp.int32, (HG_CHUNK, 1), 0)
    lb = lb_ref[...]
    gw = gw_ref[...]

    def chunk_body(c, carry):
        r0 = pl.multiple_of(c * HG_CHUNK, HG_CHUNK)
        blk = hg_ref[0, pl.ds(r0, HG_CHUNK), :]
        q = blk[:, 0:HG_WIDTH]
        f_pre = blk[:, HG_WIDTH:2 * HG_WIDTH]
        v = blk[:, 2 * HG_WIDTH:3 * HG_WIDTH]
        g_out = blk[:, 3 * HG_WIDTH:4 * HG_WIDTH]
        if first_layer:
            log_f = jnp.minimum(f_pre, 0.0) - jnp.log1p(jnp.exp(-jnp.abs(f_pre)))
            k = jax.nn.sigmoid(-f_pre)
        else:
            log_f = jnp.log(lb + (1.0 - lb) * jax.nn.sigmoid(f_pre))
            k = (1.0 - lb) * jax.nn.sigmoid(-f_pre)
        if valid_len is not None:
            live = (j * tl + r0 + row) < valid_len
            log_f = jnp.where(live, log_f, 0.0)
            k = jnp.where(live, k, 0.0)
        b = log_f
        sh = 1
        while sh < HG_CHUNK:
            b = b + jnp.where(row >= sh, pltpu.roll(b, sh, 0), 0.0)
            sh *= 2

        n_rb = HG_CHUNK // SUBLANES
        n_src = min(valid_len, HG_CHUNK) if (valid_len is not None and single_chunk) else HG_CHUNK
        for h in range(HG_HEADS):
            hs = slice(h * HG_HEAD_DIM, (h + 1) * HG_HEAD_DIM)
            qh, kh, vh, bh = q[:, hs], k[:, hs], v[:, hs], b[:, hs]
            b_last = bh[HG_CHUNK - 1:HG_CHUNK, :]
            st = st_scr[h]
            o = lax.dot_general((qh * jnp.exp(bh)).astype(BF16), st.astype(BF16),
                                (((1,), (1,)), ((), ())), preferred_element_type=F32)
            o_rb = []
            for i in range(n_rb):
                rs = slice(i * SUBLANES, (i + 1) * SUBLANES)
                q_i, b_i, acc = qh[rs], bh[rs], o[rs]
                for s in range(min((i + 1) * SUBLANES, n_src)):
                    diff = b_i - bh[s:s + 1, :]
                    if s >= i * SUBLANES:
                        diff = jnp.minimum(diff, 0.0)
                    att = jnp.sum(q_i * kh[s:s + 1, :] * jnp.exp(diff), axis=-1, keepdims=True)
                    if s >= i * SUBLANES:
                        att = jnp.where(row_col[rs] >= s, att, 0.0)
                    acc = acc + att * vh[s:s + 1, :]
                o_rb.append(acc)
            o = jnp.concatenate(o_rb, axis=0)
            k_dec = kh * jnp.exp(b_last - bh)
            upd = lax.dot_general(vh.astype(BF16), k_dec.astype(BF16),
                                  (((0,), (0,)), ((), ())), preferred_element_type=F32)
            st_scr[h] = st * jnp.exp(b_last) + upd
            o = o * lax.rsqrt(jnp.mean(o * o, axis=-1, keepdims=True) + RMS_EPS) * gw
            gh = g_out[:, hs]
            y_ref[0, pl.ds(r0, HG_CHUNK), hs] = o * (gh * jax.nn.sigmoid(gh))
        return carry

    lax.fori_loop(0, tl // HG_CHUNK, chunk_body, 0, unroll=min(HG_UNROLL, tl // HG_CHUNK))

    @pl.when(j == n_t - 1)
    def _():
        for h in range(HG_HEADS):
            sout_ref[0, h] = st_scr[h].T


def _hgrn_branch(hg, s0_layers, layer, lb, gnorm_w, tl, valid_len, first_layer):
    n_b, n_l, _ = hg.shape
    st_shape = (HG_HEADS, HG_HEAD_DIM, HG_HEAD_DIM)
    return pl.pallas_call(
        functools.partial(_hgrn_kernel, tl=tl, valid_len=valid_len, first_layer=first_layer,
                          single_chunk=n_l == HG_CHUNK),
        grid=(n_b, n_l // tl),
        in_specs=[pl.BlockSpec((1, tl, 4 * HG_WIDTH), lambda b, j: (b, j, 0)),
                  pl.BlockSpec((1, 1) + st_shape, lambda b, j: (layer, b, 0, 0, 0)),
                  pl.BlockSpec((1, HG_WIDTH), lambda b, j: (0, 0)),
                  pl.BlockSpec((1, HG_HEAD_DIM), lambda b, j: (0, 0))],
        out_specs=[pl.BlockSpec((1, tl, HG_WIDTH), lambda b, j: (b, j, 0)),
                   pl.BlockSpec((1,) + st_shape, lambda b, j: (b, 0, 0, 0))],
        out_shape=[jax.ShapeDtypeStruct((n_b, n_l, HG_WIDTH), F32),
                   jax.ShapeDtypeStruct((n_b,) + st_shape, F32)],
        scratch_shapes=[pltpu.VMEM(st_shape, F32)],
        compiler_params=_cparams("parallel", "arbitrary"),
        name="hgrn_branch",
    )(hg, s0_layers, lb.reshape(1, HG_WIDTH), gnorm_w.reshape(1, HG_HEAD_DIM))


def _mix_kernel(ys5_ref, yhg_ref, gates_ref, x_ref, gate1_ref, shift2_ref, scale2_ref,
                wglu_ref, bglu_ref, wus_ref, wuh_ref, wout_ref, g_ref, b_ref, rw_ref, rb_ref,
                x1_ref, u2_ref, topi_ref, topw_ref):
    ys = ys5_ref[0]
    glu = ys * jax.nn.sigmoid(
        jnp.dot(ys.astype(BF16), wglu_ref[...], preferred_element_type=F32) + bglu_ref[...])
    gates = jax.nn.sigmoid(gates_ref[0])
    merged = (gates[:, :D_MODEL] * jnp.dot(glu.astype(BF16), wus_ref[...], preferred_element_type=F32)
              + gates[:, D_MODEL:] * jnp.dot(yhg_ref[0].astype(BF16), wuh_ref[...],
                                             preferred_element_type=F32))
    mix = jnp.dot(merged.astype(BF16), wout_ref[...], preferred_element_type=F32)
    x1 = _layer_norm(DEEPNORM_ALPHA * x_ref[0] + gate1_ref[0] * mix, g_ref[...], b_ref[...])
    x1_ref[0] = x1
    u2 = x1 * (1.0 + scale2_ref[0]) + shift2_ref[0]
    u2_ref[0] = u2
    u_hi = u2.astype(BF16)
    u_lo = (u2 - u_hi.astype(F32)).astype(BF16)
    hh_hl = jnp.dot(u_hi, rw_ref[...], preferred_element_type=F32)
    lh = jnp.dot(u_lo, rw_ref[:, :LANES], preferred_element_type=F32)
    logits = (hh_hl[:, :LANES] + hh_hl[:, LANES:]) + lh + rb_ref[...]
    lane = lax.broadcasted_iota(jnp.int32, logits.shape, 1)
    lane_f = lane.astype(F32)
    vals, idxs = [], []
    for _ in range(TOP_K):
        m = jnp.max(logits, axis=-1, keepdims=True)
        idx = jnp.min(jnp.where(logits == m, lane_f, float(LANES)), axis=-1,
                      keepdims=True).astype(jnp.int32)
        vals.append(m)
        idxs.append(idx)
        logits = jnp.where(lane == idx, NEG_BIG, logits)
    exps = [jnp.exp(val - vals[0]) for val in vals]
    denom = exps[0] + exps[1] + exps[2] + exps[3]
    topi = jnp.full(lane.shape, -1, jnp.int32)
    topw = jnp.zeros(lane.shape, F32)
    for kk in range(TOP_K):
        topi = jnp.where(lane == kk, idxs[kk], topi)
        topw = jnp.where(lane == kk, exps[kk] / denom, topw)
    topi_ref[0] = topi
    topw_ref[0] = topw


def _mix(ys5, yhg, gates, x, gate1, shift2, scale2, lw, tm, per_row):
    n_g, n_r, _ = x.shape

    def row_spec(width):
        return pl.BlockSpec((1, tm, width), lambda g, i: (g, i, 0))

    def const(shape):
        return pl.BlockSpec(shape, lambda g, i: (0, 0))

    ms = _mod_spec(tm, per_row)
    return pl.pallas_call(
        _mix_kernel,
        grid=(n_g, n_r // tm),
        in_specs=[row_spec(S5_WIDTH), row_spec(HG_WIDTH), row_spec(2 * D_MODEL), row_spec(D_MODEL),
                  ms, ms, ms,
                  const((S5_WIDTH, S5_WIDTH)), const((1, S5_WIDTH)),
                  const((S5_WIDTH, D_MODEL)), const((HG_WIDTH, D_MODEL)),
                  const((D_MODEL, D_MODEL)), const((1, D_MODEL)), const((1, D_MODEL)),
                  const((D_MODEL, 2 * LANES)), const((1, LANES))],
        out_specs=[row_spec(D_MODEL), row_spec(D_MODEL), row_spec(LANES), row_spec(LANES)],
        out_shape=[jax.ShapeDtypeStruct((n_g, n_r, D_MODEL), F32),
                   jax.ShapeDtypeStruct((n_g, n_r, D_MODEL), F32),
                   jax.ShapeDtypeStruct((n_g, n_r, LANES), jnp.int32),
                   jax.ShapeDtypeStruct((n_g, n_r, LANES), F32)],
        compiler_params=_cparams("parallel", "parallel"),
        name="branch_mix",
    )(ys5, yhg, gates, x, gate1, shift2, scale2,
      lw['w_glu'], lw['b_glu'], lw['w_up_s5'], lw['w_up_hg'], lw['w_out'],
      lw['ln1_g'], lw['ln1_b'], lw['router_w'], lw['router_b'])


RANK_TILE = 384


def _expert_onehot(top_idx, kk):
    lane = lax.broadcasted_iota(jnp.int32, top_idx.shape, 1)
    return (top_idx[:, kk:kk + 1] == lane).astype(F32)


def _count_kernel(topi_ref, cnt_ref):
    @pl.when(pl.program_id(0) == 0)
    def _():
        cnt_ref[...] = jnp.zeros(cnt_ref.shape, F32)

    top_idx = topi_ref[...]
    acc = cnt_ref[...]
    for kk in range(TOP_K):
        acc = acc + jnp.sum(_expert_onehot(top_idx, kk), axis=0, keepdims=True)
    cnt_ref[...] = acc


def _rank_kernel(topi_ref, start_ref, dest_ref, off_scr):
    @pl.when(pl.program_id(0) == 0)
    def _():
        off_scr[...] = start_ref[...]

    top_idx = topi_ref[...]
    tm = top_idx.shape[0]
    lane = lax.broadcasted_iota(jnp.int32, top_idx.shape, 1)
    earlier = (lax.broadcasted_iota(jnp.int32, (tm, tm), 1)
               < lax.broadcasted_iota(jnp.int32, (tm, tm), 0)).astype(BF16)
    off = off_scr[...]
    dest = jnp.zeros(top_idx.shape, jnp.int32)
    for kk in range(TOP_K):
        onehot = _expert_onehot(top_idx, kk)
        before = jnp.dot(earlier, onehot.astype(BF16), preferred_element_type=F32)
        slot = jnp.sum(onehot * (before + off), axis=-1, keepdims=True)
        dest = jnp.where(lane == kk, slot.astype(jnp.int32), dest)
        off = off + jnp.sum(onehot, axis=0, keepdims=True)
    dest_ref[...] = dest
    off_scr[...] = off


def _slot_tables(top_idx_all):
    n_tok = top_idx_all.shape[0]
    n_tiles = n_tok // RANK_TILE
    tile_spec = pl.BlockSpec((RANK_TILE, LANES), lambda i: (i, 0))
    lane_spec = pl.BlockSpec((1, LANES), lambda i: (0, 0))
    counts = pl.pallas_call(
        _count_kernel, grid=(n_tiles,), in_specs=[tile_spec], out_specs=lane_spec,
        out_shape=jax.ShapeDtypeStruct((1, LANES), F32),
        compiler_params=_cparams("arbitrary"), name="expert_count",
    )(top_idx_all)
    counts = counts[0, :N_EXPERTS].astype(jnp.int32)
    padded = (counts + MOE_BLOCK - 1) // MOE_BLOCK * MOE_BLOCK
    pad_end = jnp.cumsum(padded)
    pad_start = pad_end - padded
    start_row = jnp.zeros((1, LANES), F32).at[0, :N_EXPERTS].set(pad_start.astype(F32))
    dest = pl.pallas_call(
        _rank_kernel, grid=(n_tiles,), in_specs=[tile_spec, lane_spec], out_specs=tile_spec,
        out_shape=jax.ShapeDtypeStruct((n_tok, LANES), jnp.int32),
        scratch_shapes=[pltpu.VMEM((1, LANES), F32)],
        compiler_params=_cparams("arbitrary"), name="expert_rank",
    )(top_idx_all, start_row)
    n_blocks = -(-(n_tok * TOP_K) // MOE_BLOCK) + N_EXPERTS
    block_lo = jnp.arange(n_blocks, dtype=jnp.int32) * MOE_BLOCK
    block_e = jnp.minimum(jnp.sum((pad_end[None, :] <= block_lo[:, None]).astype(jnp.int32), axis=1),
                          N_EXPERTS - 1).astype(jnp.int32)
    n_used = (pad_end[-1] // MOE_BLOCK).astype(jnp.int32).reshape(1)
    return dest[:, :TOP_K].reshape(-1), pad_end.astype(jnp.int32), block_e, n_used, n_blocks


def _dispatch_kernel(dest_ref, pad_end_ref, n_used_ref, up_ref, us_ref, xb_hbm, zero_scr, sem,
                     *, tm_p, n_tiles_p, n_blocks):
    i = pl.program_id(0)

    def zero_block(row0):
        return pltpu.make_async_copy(zero_scr, xb_hbm.at[pl.ds(row0, MOE_BLOCK)], sem)

    @pl.when(i == 0)
    def _():
        zero_scr[...] = jnp.zeros(zero_scr.shape, F32)

        def pad_block(e):
            start = pad_end_ref[e - 1] if e else 0
            end = pad_end_ref[e]
            return end > start, pl.multiple_of(jnp.maximum(end - MOE_BLOCK, 0), MOE_BLOCK)

        def tail_start(blk, carry):
            zero_block(pl.multiple_of(blk * MOE_BLOCK, MOE_BLOCK)).start()
            return carry

        def tail_wait(blk, carry):
            zero_block(pl.multiple_of(blk * MOE_BLOCK, MOE_BLOCK)).wait()
            return carry

        for e in range(N_EXPERTS):
            live, row0 = pad_block(e)
            pl.when(live)(lambda row0=row0: zero_block(row0).start())
        lax.fori_loop(n_used_ref[0], n_blocks, tail_start, 0)
        for e in range(N_EXPERTS):
            live, row0 = pad_block(e)
            pl.when(live)(lambda row0=row0: zero_block(row0).wait())
        lax.fori_loop(n_used_ref[0], n_blocks, tail_wait, 0)

    def scatter(src_ref, n_rows, first_tok):
        def issue(r, carry):
            for kk in range(TOP_K):
                slot = dest_ref[(first_tok + r) * TOP_K + kk]
                pltpu.make_async_copy(src_ref.at[pl.ds(r, 1)], xb_hbm.at[pl.ds(slot, 1)], sem).start()
            return carry

        lax.fori_loop(0, n_rows, issue, 0, unroll=DMA_ISSUE_UNROLL)
        for kk in range(TOP_K):
            pltpu.make_async_copy(src_ref, xb_hbm.at[pl.ds(0, n_rows)], sem).wait()

    @pl.when(i < n_tiles_p)
    def _():
        scatter(up_ref, tm_p, i * tm_p)

    @pl.when(i == n_tiles_p)
    def _():
        scatter(us_ref, us_ref.shape[0], n_tiles_p * tm_p)


def _dispatch(dest, pad_end, n_used, u_prompt, u_sample, n_blocks, tm_p):
    n_tiles_p = u_prompt.shape[0] // tm_p
    n_s = u_sample.shape[0]
    grid_spec = pltpu.PrefetchScalarGridSpec(
        num_scalar_prefetch=3, grid=(n_tiles_p + 1,),
        in_specs=[pl.BlockSpec((tm_p, D_MODEL), lambda i, d, p, nu: (jnp.minimum(i, n_tiles_p - 1), 0)),
                  pl.BlockSpec((n_s, D_MODEL), lambda i, d, p, nu: (0, 0))],
        out_specs=pl.BlockSpec(memory_space=pl.ANY),
        scratch_shapes=[pltpu.VMEM((MOE_BLOCK, D_MODEL), F32), pltpu.SemaphoreType.DMA(())])
    return pl.pallas_call(
        functools.partial(_dispatch_kernel, tm_p=tm_p, n_tiles_p=n_tiles_p, n_blocks=n_blocks),
        grid_spec=grid_spec,
        out_shape=jax.ShapeDtypeStruct((n_blocks * MOE_BLOCK, D_MODEL), F32),
        compiler_params=_cparams("arbitrary"),
        name="moe_dispatch",
    )(dest, pad_end, n_used, u_prompt, u_sample)


def _moe_kernel(block_e_ref, n_used_ref, x_ref, wgu_ref, bgu_ref, wd_ref, bd_ref, y_ref,
                wgu_bf, wd_bf):
    i = pl.program_id(0)
    n_used = n_used_ref[0]
    e = block_e_ref[i]
    e_prev = block_e_ref[jnp.maximum(i - 1, 0)]

    @pl.when(jnp.logical_and(i < n_used, jnp.logical_or(i == 0, e != e_prev)))
    def _():
        wgu_bf[...] = wgu_ref[0, 0].astype(BF16)
        wd_bf[...] = wd_ref[0, 0].astype(BF16)

    @pl.when(i < n_used)
    def _():
        h = jnp.dot(x_ref[...].astype(BF16), wgu_bf[...], preferred_element_type=F32) + bgu_ref[0, 0]
        gate = jnp.minimum(h[:, :D_EXPERT], SWIGLU_LIMIT)
        up = jnp.clip(h[:, D_EXPERT:], -SWIGLU_LIMIT, SWIGLU_LIMIT)
        act = gate * jax.nn.sigmoid(SWIGLU_ALPHA * gate) * (up + 1.0)
        y_ref[...] = jnp.dot(act.astype(BF16), wd_bf[...], preferred_element_type=F32) + bd_ref[0, 0]

    @pl.when(i >= n_used)
    def _():
        y_ref[...] = jnp.zeros(y_ref.shape, F32)


def _moe_experts(xb, block_e, n_used, layer, w_gate_up, b_gate_up, w_down, b_down):
    n_blocks = block_e.shape[0]

    def expert_spec(shape):
        return pl.BlockSpec((1, 1) + shape, lambda i, be, nu: (layer, be[i], 0, 0))

    grid_spec = pltpu.PrefetchScalarGridSpec(
        num_scalar_prefetch=2,
        grid=(n_blocks,),
        in_specs=[pl.BlockSpec((MOE_BLOCK, D_MODEL), lambda i, be, nu: (jnp.minimum(i, nu[0] - 1), 0)),
                  expert_spec((D_MODEL, 2 * D_EXPERT)), expert_spec((1, 2 * D_EXPERT)),
                  expert_spec((D_EXPERT, D_MODEL)), expert_spec((1, D_MODEL))],
        out_specs=pl.BlockSpec((MOE_BLOCK, D_MODEL), lambda i, be, nu: (i, 0)),
        scratch_shapes=[pltpu.VMEM((D_MODEL, 2 * D_EXPERT), BF16),
                        pltpu.VMEM((D_EXPERT, D_MODEL), BF16)],
    )
    return pl.pallas_call(
        _moe_kernel,
        grid_spec=grid_spec,
        out_shape=jax.ShapeDtypeStruct((n_blocks * MOE_BLOCK, D_MODEL), F32),
        compiler_params=_cparams("arbitrary"),
        name="moe_experts",
    )(block_e, n_used, xb, w_gate_up,
      b_gate_up.reshape(DEPTH, N_EXPERTS, 1, 2 * D_EXPERT),
      w_down, b_down.reshape(DEPTH, N_EXPERTS, 1, D_MODEL))


def _combine_kernel(dest_ref, yb_hbm, x1_ref, topw_ref, gate2_ref, g_ref, b_ref, o_ref, ybuf, sem,
                    *, tm, tok_offset):
    g = pl.program_id(0)
    i = pl.program_id(1)
    base = (tok_offset + (g * pl.num_programs(1) + i) * tm) * TOP_K

    def issue(r, carry):
        for kk in range(TOP_K):
            slot = dest_ref[base + r * TOP_K + kk]
            pltpu.make_async_copy(yb_hbm.at[pl.ds(slot, 1)], ybuf.at[kk, pl.ds(r, 1)], sem).start()
        return carry

    lax.fori_loop(0, tm, issue, 0, unroll=DMA_ISSUE_UNROLL)
    for kk in range(TOP_K):
        pltpu.make_async_copy(yb_hbm.at[pl.ds(0, tm)], ybuf.at[kk], sem).wait()
    topw = topw_ref[0]
    ffn = ((topw[:, 0:1] * ybuf[0] + topw[:, 1:2] * ybuf[1])
           + (topw[:, 2:3] * ybuf[2] + topw[:, 3:4] * ybuf[3]))
    o_ref[0] = _layer_norm(DEEPNORM_ALPHA * x1_ref[0] + gate2_ref[0] * ffn, g_ref[...], b_ref[...])


def _combine(dest, yb, x1, topw, gate2, ln_g, ln_b, tm, per_row, tok_offset):
    n_g, n_r, _ = x1.shape
    ms = (pl.BlockSpec((1, tm, D_MODEL), lambda g, i, d: (g, i, 0)) if per_row
          else pl.BlockSpec((1, 1, D_MODEL), lambda g, i, d: (g, 0, 0)))
    grid_spec = pltpu.PrefetchScalarGridSpec(
        num_scalar_prefetch=1,
        grid=(n_g, n_r // tm),
        in_specs=[pl.BlockSpec(memory_space=pl.ANY),
                  pl.BlockSpec((1, tm, D_MODEL), lambda g, i, d: (g, i, 0)),
                  pl.BlockSpec((1, tm, LANES), lambda g, i, d: (g, i, 0)),
                  ms,
                  pl.BlockSpec((1, D_MODEL), lambda g, i, d: (0, 0)),
                  pl.BlockSpec((1, D_MODEL), lambda g, i, d: (0, 0))],
        out_specs=pl.BlockSpec((1, tm, D_MODEL), lambda g, i, d: (g, i, 0)),
        scratch_shapes=[pltpu.VMEM((TOP_K, tm, D_MODEL), F32), pltpu.SemaphoreType.DMA(())],
    )
    return pl.pallas_call(
        functools.partial(_combine_kernel, tm=tm, tok_offset=tok_offset),
        grid_spec=grid_spec,
        out_shape=jax.ShapeDtypeStruct(x1.shape, F32),
        compiler_params=_cparams("arbitrary", "arbitrary"),
        name="moe_combine",
    )(dest, yb, x1, topw, gate2, ln_g, ln_b)


def kernel(x_prompt, x_sample, state_s5_re, state_s5_im, state_hgrn, c_prompt, c_sample,
           w_ada, b_ada, w_in, s5_lambda_re, s5_lambda_im, s5_log_dt, s5_b_re, s5_b_im,
           s5_c_re, s5_c_im, s5_d, w_glu, b_glu, w_up_s5, hgrn_lb_logits, hgrn_gnorm_w,
           w_up_hg, w_out, ln1_g, ln1_b, router_w, router_b, w_gate_up, b_gate_up,
           w_down, b_down, ln2_g, ln2_b):
    n_p, seq, _ = x_prompt.shape
    n_s = x_sample.shape[0]
    n_tok_p = n_p * seq

    lb_p = jax.nn.softmax(hgrn_lb_logits.astype(F32), axis=0)
    lb_all = jnp.cumsum(lb_p, axis=0) - lb_p[0]

    mod = _modulation(jnp.concatenate([c_prompt, c_sample], axis=0), w_ada, b_ada)

    def mods(layer, which):
        sl = slice(which * D_MODEL, (which + 1) * D_MODEL)
        return (mod[layer, :n_p, sl].reshape(n_p, 1, D_MODEL),
                mod[layer, n_p:, sl].reshape(1, n_s, D_MODEL))

    xp = x_prompt
    xs = x_sample.reshape(1, n_s, D_MODEL)
    zero_s5 = jnp.zeros((n_p, S5_FLAT), F32)
    zero_hg = jnp.zeros((1, n_p, HG_HEADS, HG_HEAD_DIM, HG_HEAD_DIM), F32)
    outs = {k: [] for k in ('p_re', 'p_im', 'p_hg', 's_re', 's_im', 's_hg')}

    for l in range(DEPTH):
        shift1, scale1, gate1, shift2, scale2, gate2 = [mods(l, w) for w in range(6)]
        w_in_bf = w_in[l].astype(BF16)
        bb, cc, a, d = _s5_params(s5_lambda_re[l], s5_lambda_im[l], s5_log_dt[l], s5_b_re[l],
                                  s5_b_im[l], s5_c_re[l], s5_c_im[l], s5_d[l])
        router_w_pad = jnp.zeros((D_MODEL, LANES), F32).at[:, :N_EXPERTS].set(router_w[l])
        router_w_hi = router_w_pad.astype(BF16)
        router_w_lo = (router_w_pad - router_w_hi.astype(F32)).astype(BF16)
        router_b_pad = jnp.full((1, LANES), NEG_BIG, F32).at[0, :N_EXPERTS].set(router_b[l])
        lw = {'w_glu': w_glu[l].astype(BF16), 'b_glu': b_glu[l].reshape(1, S5_WIDTH),
              'w_up_s5': w_up_s5[l].astype(BF16), 'w_up_hg': w_up_hg[l].astype(BF16),
              'w_out': w_out[l].astype(BF16), 'ln1_g': ln1_g[l].reshape(1, D_MODEL),
              'ln1_b': ln1_b[l].reshape(1, D_MODEL), 'router_w': jnp.concatenate([router_w_hi, router_w_lo], axis=1),
              'router_b': router_b_pad}
        first = l == 0

        us5_p, hg_p, gates_p = _input_proj(xp, shift1[0], scale1[0], w_in_bf, 256, False)
        ys5_p, pre, pim = _s5_branch(us5_p, zero_s5, zero_s5, bb, cc, a, d, 64)
        yhg_p, phg = _hgrn_branch(hg_p, zero_hg, 0, lb_all[l], hgrn_gnorm_w[l], 256, None, first)
        x1_p, u2_p, ti_p, tw_p = _mix(ys5_p, yhg_p, gates_p, xp, gate1[0], shift2[0], scale2[0],
                                      lw, 256, False)
        us5_s, hg_s, gates_s = _input_proj(xs, shift1[1], scale1[1], w_in_bf, n_s, True)
        ys5_s, sre, sim = _s5_branch(us5_s.reshape(n_s, 1, S5_WIDTH),
                                     state_s5_re[l].reshape(n_s, S5_FLAT),
                                     state_s5_im[l].reshape(n_s, S5_FLAT), bb, cc, a, d, 1)
        hg_s_pad = jnp.pad(hg_s.reshape(n_s, 1, 4 * HG_WIDTH), ((0, 0), (0, HG_CHUNK - 1), (0, 0)))
        yhg_s, shg = _hgrn_branch(hg_s_pad, state_hgrn, l, lb_all[l], hgrn_gnorm_w[l],
                                  HG_CHUNK, 1, first)
        x1_s, u2_s, ti_s, tw_s = _mix(ys5_s.reshape(1, n_s, S5_WIDTH),
                                      yhg_s[:, 0, :].reshape(1, n_s, HG_WIDTH), gates_s, xs,
                                      gate1[1], shift2[1], scale2[1], lw, n_s, True)

        top_idx_all = jnp.concatenate([ti_p.reshape(n_tok_p, LANES), ti_s.reshape(n_s, LANES)], axis=0)
        dest, pad_end, block_e, n_used, n_blocks = _slot_tables(top_idx_all)
        xb = _dispatch(dest, pad_end, n_used, u2_p.reshape(n_tok_p, D_MODEL),
                       u2_s.reshape(n_s, D_MODEL), n_blocks, 512)
        yb = _moe_experts(xb, block_e, n_used, l, w_gate_up, b_gate_up, w_down, b_down)
        g2, b2 = ln2_g[l].reshape(1, D_MODEL), ln2_b[l].reshape(1, D_MODEL)
        xp = _combine(dest, yb, x1_p, tw_p, gate2[0], g2, b2, 256, False, 0)
        xs = _combine(dest, yb, x1_s, tw_s, gate2[1], g2, b2, n_s, True, n_tok_p)

        outs['p_re'].append(pre.reshape(n_p, S5_GROUPS, S5_STATE))
        outs['p_im'].append(pim.reshape(n_p, S5_GROUPS, S5_STATE))
        outs['p_hg'].append(phg)
        outs['s_re'].append(sre.reshape(n_s, S5_GROUPS, S5_STATE))
        outs['s_im'].append(sim.reshape(n_s, S5_GROUPS, S5_STATE))
        outs['s_hg'].append(shg)

    return (xp, xs.reshape(n_s, 1, D_MODEL),
            jnp.stack(outs['p_re']), jnp.stack(outs['p_im']), jnp.stack(outs['p_hg']),
            jnp.stack(outs['s_re']), jnp.stack(outs['s_im']), jnp.stack(outs['s_hg']))
```
